```python
import math
import jax, jax.numpy as jnp
from jax import lax
import numpy as np

D_MODEL = 1024
BATCH = 8
SEQ = 2048
DEPTH = 2

N_HEADS_A = 4
HEAD_DIM_A = 128
WIDTH_A = N_HEADS_A * HEAD_DIM_A
CONV_A = 4
CHUNK = 64
SSM_WIDTH = 512
SSM_GROUP = 16
N_GROUPS = SSM_WIDTH // SSM_GROUP
SSM_STATE = 64
D_IN_AB = 4 * WIDTH_A + 2 * N_HEADS_A + SSM_WIDTH
D_MIX_AB = WIDTH_A + SSM_WIDTH
POOL_WINDOWS = (2, 4, 8, 16)
POOL_GROUP = D_MODEL // len(POOL_WINDOWS)
MEM_LEN = 256
N_HEADS_X = 4
HEAD_DIM_X = D_MODEL // N_HEADS_X
D_FF = 2816
CONV_FFN = 3
RMS_EPS = 1e-6
N_EVEN = (DEPTH + 1) // 2
N_ODD = DEPTH // 2

kernel_name = "hybrid_deltanet_s5_pool_decoder"


def rms_norm(x, g):
    xf = x.astype(jnp.float32)
    y = xf * lax.rsqrt(jnp.mean(xf * xf, axis=-1, keepdims=True) + RMS_EPS)
    return (y * g.astype(jnp.float32)).astype(x.dtype)


def causal_dwconv(x, w):
    k_w = w.shape[0]
    t = x.shape[1]
    xp = jnp.pad(x, ((0, 0), (k_w - 1, 0), (0, 0)))
    return sum(xp[:, i:i + t, :] * w[i] for i in range(k_w))


def l2_normalize(x):
    return x * lax.rsqrt(jnp.sum(x * x, axis=-1, keepdims=True) + 1e-6)


def gated_delta_rule_chunked(q, k, v, g, beta):
    b, h, t, dk = q.shape
    dv = v.shape[-1]
    n = t // CHUNK
    q = q * (dk ** -0.5)
    q, k, v = (a.reshape(b, h, n, CHUNK, a.shape[-1]) for a in (q, k, v))
    g = jnp.cumsum(g.reshape(b, h, n, CHUNK), axis=-1)
    beta = beta.reshape(b, h, n, CHUNK)
    causal = jnp.tril(jnp.ones((CHUNK, CHUNK), bool))
    strict = jnp.tril(jnp.ones((CHUNK, CHUNK), bool), -1)
    diff = g[..., :, None] - g[..., None, :]
    decay = jnp.where(causal, jnp.exp(jnp.where(causal, diff, 0.0)), 0.0)
    k_beta = k * beta[..., None]
    a_mat = jnp.where(strict, jnp.einsum('bhncd,bhnsd->bhncs', k_beta, k) * decay, 0.0)
    eye = jnp.eye(CHUNK, dtype=jnp.float32)
    rhs = jnp.concatenate([v * beta[..., None], k_beta * jnp.exp(g)[..., None]], axis=-1)
    sol = lax.linalg.triangular_solve(a_mat + eye, rhs, left_side=True, lower=True,
                                      unit_diagonal=True)
    u, w = sol[..., :dv], sol[..., dv:]
    qk = jnp.where(causal, jnp.einsum('bhncd,bhnsd->bhncs', q, k) * decay, 0.0)

    def step(state, xs):
        q_c, k_c, u_c, w_c, qk_c, g_c = xs
        v_new = u_c - jnp.einsum('bhck,bhkv->bhcv', w_c, state)
        o = (jnp.einsum('bhck,bhkv->bhcv', q_c * jnp.exp(g_c)[..., None], state)
             + jnp.einsum('bhcs,bhsv->bhcv', qk_c, v_new))
        g_last = g_c[..., -1:]
        state = (state * jnp.exp(g_last)[..., None]
                 + jnp.einsum('bhck,bhcv->bhkv', k_c * jnp.exp(g_last - g_c)[..., None], v_new))
        return state, o

    xs = tuple(jnp.moveaxis(a, 2, 0) for a in (q, k, u, w, qk, g))
    s0 = jnp.zeros((b, h, dk, dv), jnp.float32)
    _, o = lax.scan(step, s0, xs)
    return jnp.moveaxis(o, 0, 2).reshape(b, h, t, dv)


def s5_ssm(u, lam_re, lam_im, b_re, b_im, c_re, c_im, d, log_dt):
    uf = u.astype(jnp.float32)
    dt = jnp.exp(log_dt.astype(jnp.float32))[:, None]
    lr, li = lam_re.astype(jnp.float32), lam_im.astype(jnp.float32)
    mag = jnp.exp(lr * dt)
    ang = li * dt
    lb_re, lb_im = mag * jnp.cos(ang), mag * jnp.sin(ang)
    den = lr * lr + li * li
    nr, ni = lb_re - 1.0, lb_im
    coef_re = (nr * lr + ni * li) / den
    coef_im = (ni * lr - nr * li) / den
    br, bi = b_re.astype(jnp.float32), b_im.astype(jnp.float32)
    bb_re = coef_re[..., None] * br - coef_im[..., None] * bi
    bb_im = coef_re[..., None] * bi + coef_im[..., None] * br
    bu_re = jnp.einsum('gph,btgh->btgp', bb_re, uf)
    bu_im = jnp.einsum('gph,btgh->btgp', bb_im, uf)
    a_re = jnp.broadcast_to(lb_re, bu_re.shape)
    a_im = jnp.broadcast_to(lb_im, bu_re.shape)

    def combine(e1, e2):
        a1r, a1i, b1r, b1i = e1
        a2r, a2i, b2r, b2i = e2
        return (a1r * a2r - a1i * a2i,
                a1r * a2i + a1i * a2r,
                a2r * b1r - a2i * b1i + b2r,
                a2r * b1i + a2i * b1r + b2i)

    _, _, xr, xi = lax.associative_scan(combine, (a_re, a_im, bu_re, bu_im), axis=1)
    y = (jnp.einsum('ghp,btgp->btgh', c_re.astype(jnp.float32), xr)
         - jnp.einsum('ghp,btgp->btgh', c_im.astype(jnp.float32), xi)
         + d.astype(jnp.float32) * uf)
    return y


def hybrid_delta_ssm_mixer(xn, w_in, conv_qkv, a_log, dt_bias, onorm_g,
                           lam_re, lam_im, b_re, b_im, c_re, c_im, ssm_d, log_dt,
                           w_glu, b_glu, w_out):
    b, t, _ = xn.shape
    h = xn @ w_in
    qkv, gate, beta_logit, alpha_logit, u = jnp.split(
        h, [3 * WIDTH_A, 4 * WIDTH_A, 4 * WIDTH_A + N_HEADS_A, 4 * WIDTH_A + 2 * N_HEADS_A],
        axis=-1)
    qkv = jax.nn.silu(causal_dwconv(qkv, conv_qkv)).astype(jnp.float32)
    q, k, v = (a.reshape(b, t, N_HEADS_A, HEAD_DIM_A).transpose(0, 2, 1, 3)
               for a in jnp.split(qkv, 3, axis=-1))
    q, k = l2_normalize(q), l2_normalize(k)
    beta = jax.nn.sigmoid(beta_logit.astype(jnp.float32)).transpose(0, 2, 1)
    g = (-jnp.exp(a_log.astype(jnp.float32))
         * jax.nn.softplus(alpha_logit.astype(jnp.float32) + dt_bias.astype(jnp.float32))
         ).transpose(0, 2, 1)
    o = gated_delta_rule_chunked(q, k, v, g, beta).transpose(0, 2, 1, 3)
    o = rms_norm(o, onorm_g) * jax.nn.silu(
        gate.reshape(b, t, N_HEADS_A, HEAD_DIM_A).astype(jnp.float32))
    y_a = o.reshape(b, t, WIDTH_A)
    y = s5_ssm(u.reshape(b, t, N_GROUPS, SSM_GROUP), lam_re, lam_im, b_re, b_im,
               c_re, c_im, ssm_d, log_dt)
    y = jax.nn.gelu(y.reshape(b, t, SSM_WIDTH))
    y_b = y * jax.nn.sigmoid(y @ w_glu.astype(jnp.float32) + b_glu.astype(jnp.float32))
    mixed = jnp.concatenate([y_a, y_b], axis=-1).astype(xn.dtype)
    return mixed @ w_out


def multiscale_pool_mixer(xn, pool_w, pool_scale):
    b, t, d = xn.shape
    xf = xn.astype(jnp.float32)
    cs = jnp.pad(lax.cumsum(xf, axis=1), ((0, 0), (1, 0), (0, 0)))
    pos_count = jnp.arange(1, t + 1, dtype=jnp.float32)[:, None]
    outs = []
    for gi, win in enumerate(POOL_WINDOWS):
        sl = slice(gi * POOL_GROUP, (gi + 1) * POOL_GROUP)
        csg = cs[..., sl]
        lower = jnp.pad(csg, ((0, 0), (win, 0), (0, 0)))[:, 1:t + 1]
        mean = (csg[:, 1:] - lower) / jnp.minimum(pos_count, float(win))
        outs.append(jnp.einsum('btc,ce->bte', mean - xf[..., sl],
                               pool_w[gi].astype(jnp.float32)))
    return (jnp.concatenate(outs, axis=-1) * pool_scale.astype(jnp.float32)).astype(xn.dtype)


def mem_cross_attention(xn, mem_n, wq, wkv, wo):
    b, t, _ = xn.shape
    m = mem_n.shape[1]
    q = (xn @ wq).reshape(b, t, N_HEADS_X, HEAD_DIM_X)
    kv = (mem_n @ wkv).reshape(b, m, 2, N_HEADS_X, HEAD_DIM_X)
    k, v = kv[:, :, 0], kv[:, :, 1]
    s = jnp.einsum('bthd,bmhd->bhtm', q, k).astype(jnp.float32) * (HEAD_DIM_X ** -0.5)
    p = jax.nn.softmax(s, axis=-1).astype(v.dtype)
    o = jnp.einsum('bhtm,bmhd->bthd', p, v).reshape(b, t, N_HEADS_X * HEAD_DIM_X)
    return o @ wo


def conv_ffn(xn, w_up, conv_w, w_down):
    h = causal_dwconv(xn @ w_up, conv_w)
    gate, val = jnp.split(h, 2, axis=-1)
    return (jax.nn.silu(gate) * val) @ w_down


def setup_inputs(seed: int = 0) -> dict:
    key = jax.random.key(seed)
    ks = iter(jax.random.split(key, 48))

    def nrm(shape, scale):
        return jax.random.normal(next(ks), shape, jnp.float32) * scale

    def gain(shape):
        return 1.0 + nrm(shape, 0.05)

    E, O, L, D = N_EVEN, N_ODD, DEPTH, D_MODEL
    x = nrm((BATCH, SEQ, D), 1.0)
    mem = nrm((BATCH, MEM_LEN, D), 1.0)
    norm_mix_g = gain((L, D))
    norm_xa_g = gain((L, D))
    norm_ffn_g = gain((L, D))
    norm_mem_g = gain((D,))
    norm_final_g = gain((D,))
    w_in_ab = nrm((E, D, D_IN_AB), D ** -0.5)
    conv_qkv_a = nrm((E, CONV_A, 3 * WIDTH_A), CONV_A ** -0.5)
    a_log_a = jnp.log(jax.random.uniform(next(ks), (E, N_HEADS_A), jnp.float32, 1.0, 16.0))
    dt0 = jnp.exp(jax.random.uniform(next(ks), (E, N_HEADS_A), jnp.float32,
                                     math.log(1e-3), math.log(1e-1)))
    dt_bias_a = dt0 + jnp.log(-jnp.expm1(-dt0))
    onorm_g_a = gain((E, HEAD_DIM_A))
    ssm_lambda_re = -0.5 + nrm((E, N_GROUPS, SSM_STATE), 0.01)
    ssm_lambda_im = jnp.broadcast_to(
        math.pi * jnp.arange(SSM_STATE, dtype=jnp.float32), (E, N_GROUPS, SSM_STATE)
    ) + nrm((E, N_GROUPS, SSM_STATE), 0.001)
    ssm_b_re = nrm((E, N_GROUPS, SSM_STATE, SSM_GROUP), (2 * SSM_GROUP) ** -0.5)
    ssm_b_im = nrm((E, N_GROUPS, SSM_STATE, SSM_GROUP), (2 * SSM_GROUP) ** -0.5)
    ssm_c_re = nrm((E, N_GROUPS, SSM_GROUP, SSM_STATE), (2 * SSM_STATE) ** -0.5)
    ssm_c_im = nrm((E, N_GROUPS, SSM_GROUP, SSM_STATE), (2 * SSM_STATE) ** -0.5)
    ssm_d = nrm((E, N_GROUPS, SSM_GROUP), 1.0)
    ssm_log_dt = jax.random.uniform(next(ks), (E, N_GROUPS), jnp.float32,
                                    math.log(1e-3), math.log(1e-1))
    w_glu_b = nrm((E, SSM_WIDTH, SSM_WIDTH), SSM_WIDTH ** -0.5)
    b_glu_b = nrm((E, SSM_WIDTH), 0.01)
    w_out_ab = nrm((E, D_MIX_AB, D), D_MIX_AB ** -0.5)
    pool_w = nrm((O, len(POOL_WINDOWS), POOL_GROUP, POOL_GROUP), POOL_GROUP ** -0.5)
    pool_scale = 1.0 + nrm((O, D), 0.1)
    xa_wq = nrm((L, D, D), D ** -0.5)
    xa_wkv = nrm((L, D, 2 * D), D ** -0.5)
    xa_wo = nrm((L, D, D), D ** -0.5)
    ffn_w_up = nrm((L, D, 2 * D_FF), D ** -0.5)
    ffn_conv = nrm((L, CONV_FFN, 2 * D_FF), 0.3) + jnp.array([0.0, 0.0, 1.0], jnp.float32)[None, :, None]
    ffn_w_down = nrm((L, D_FF, D), D_FF ** -0.5)
    return {
        "x": x, "mem": mem,
        "norm_mix_g": norm_mix_g, "norm_xa_g": norm_xa_g, "norm_ffn_g": norm_ffn_g,
        "norm_mem_g": norm_mem_g, "norm_final_g": norm_final_g,
        "w_in_ab": w_in_ab, "conv_qkv_a": conv_qkv_a, "a_log_a": a_log_a,
        "dt_bias_a": dt_bias_a, "onorm_g_a": onorm_g_a,
        "ssm_lambda_re": ssm_lambda_re, "ssm_lambda_im": ssm_lambda_im,
        "ssm_b_re": ssm_b_re, "ssm_b_im": ssm_b_im, "ssm_c_re": ssm_c_re, "ssm_c_im": ssm_c_im,
        "ssm_d": ssm_d, "ssm_log_dt": ssm_log_dt, "w_glu_b": w_glu_b, "b_glu_b": b_glu_b,
        "w_out_ab": w_out_ab,
        "pool_w": pool_w, "pool_scale": pool_scale,
        "xa_wq": xa_wq, "xa_wkv": xa_wkv, "xa_wo": xa_wo,
        "ffn_w_up": ffn_w_up, "ffn_conv": ffn_conv, "ffn_w_down": ffn_w_down,
    }


def reference(x, mem, norm_mix_g, norm_xa_g, norm_ffn_g, norm_mem_g, norm_final_g,
              w_in_ab, conv_qkv_a, a_log_a, dt_bias_a, onorm_g_a,
              ssm_lambda_re, ssm_lambda_im, ssm_b_re, ssm_b_im, ssm_c_re, ssm_c_im,
              ssm_d, ssm_log_dt, w_glu_b, b_glu_b, w_out_ab,
              pool_w, pool_scale, xa_wq, xa_wkv, xa_wo,
              ffn_w_up, ffn_conv, ffn_w_down):
    mem_n = rms_norm(mem, norm_mem_g)
    for layer in range(DEPTH):
        xn = rms_norm(x, norm_mix_g[layer])
        if layer % 2 == 0:
            e = layer // 2
            mix = hybrid_delta_ssm_mixer(
                xn, w_in_ab[e], conv_qkv_a[e], a_log_a[e], dt_bias_a[e], onorm_g_a[e],
                ssm_lambda_re[e], ssm_lambda_im[e], ssm_b_re[e], ssm_b_im[e],
                ssm_c_re[e], ssm_c_im[e], ssm_d[e], ssm_log_dt[e],
                w_glu_b[e], b_glu_b[e], w_out_ab[e])
        else:
            o = layer // 2
            mix = multiscale_pool_mixer(xn, pool_w[o], pool_scale[o])
        x = x + mix.astype(x.dtype)
        x = x + mem_cross_attention(rms_norm(x, norm_xa_g[layer]), mem_n,
                                    xa_wq[layer], xa_wkv[layer], xa_wo[layer]).astype(x.dtype)
        x = x + conv_ffn(rms_norm(x, norm_ffn_g[layer]),
                         ffn_w_up[layer], ffn_conv[layer], ffn_w_down[layer]).astype(x.dtype)
    return rms_norm(x, norm_final_g)
```

```python
import functools
import math

import jax
import jax.numpy as jnp
from jax import lax
from jax.experimental import pallas as pl
from jax.experimental.pallas import tpu as pltpu

F32 = jnp.float32
BF16 = jnp.bfloat16

NB = 8
D_MODEL = 1024
LANE = 128
N_HEADS_A = 4
HEAD_DIM_A = 128
WIDTH_A = 512
CONV_A = 4
CHUNK = 64
SSM_WIDTH = 512
SSM_GROUP = 16
N_GROUPS = 32
SSM_STATE = 64
N_SUPER = 4
SUPER_STATE = 8 * SSM_STATE
POOL_WINDOWS = (2, 4, 8, 16)
POOL_GROUP = 256
MEM_LEN = 256
N_HEADS_X = 4
HEAD_DIM_X = 256
D_FF = 2816
CONV_FFN = 3
FF_CHUNK = 256
RMS_EPS = 1e-6
VMEM_LIMIT = 56 * 1024 * 1024


def _dot(a, b):
    return jnp.dot(a.astype(BF16), b.astype(BF16), preferred_element_type=F32)


def _dot_nt(a, b):
    return lax.dot_general(a.astype(BF16), b.astype(BF16), (((1,), (1,)), ((), ())),
                           preferred_element_type=F32)


def _rms(x, g):
    return x * lax.rsqrt(jnp.mean(x * x, axis=-1, keepdims=True) + RMS_EPS) * g


def _sigmoid(x):
    return 1.0 / (1.0 + jnp.exp(-x))


def _silu(x):
    return x * _sigmoid(x)


def _params(n_axes=1):
    return pltpu.CompilerParams(dimension_semantics=("arbitrary",) * n_axes,
                                vmem_limit_bytes=VMEM_LIMIT)


def _full(shape):
    return pl.BlockSpec(shape, lambda *_: (0,) * len(shape))


def _memkv_kernel(mem_ref, g_ref, wkv_ref, k_ref, v_ref):
    mn = _rms(mem_ref[...], g_ref[...])
    kv = _dot(mn, wkv_ref[...])
    k_ref[...] = kv[:, :D_MODEL].astype(BF16)
    v_ref[...] = kv[:, D_MODEL:].astype(BF16)


def _memkv(mem, g, wkv):
    n_layers = wkv.shape[0]
    out = jax.ShapeDtypeStruct((n_layers, NB, MEM_LEN, D_MODEL), BF16)
    return pl.pallas_call(
        _memkv_kernel,
        grid=(n_layers, NB),
        in_specs=[pl.BlockSpec((None, MEM_LEN, D_MODEL), lambda l, b: (b, 0, 0)),
                  pl.BlockSpec((1, D_MODEL), lambda l, b: (0, 0)),
                  pl.BlockSpec((None, D_MODEL, 2 * D_MODEL), lambda l, b: (l, 0, 0))],
        out_specs=[pl.BlockSpec((None, None, MEM_LEN, D_MODEL), lambda l, b: (l, b, 0, 0))] * 2,
        out_shape=[out, out],
        compiler_params=_params(2),
        name="memkv",
    )(mem, g, wkv)


def _mix_in_kernel(x_ref, g_ref, wqkv_ref, wgate_ref, wba_ref, wu_ref, conv_ref, alog_ref, dtb_ref,
                   q_ref, k_ref, v_ref, bg_ref, gate_ref, u_ref, carry_ref):
    tm = x_ref.shape[0]
    hist = (CONV_A - 1) * NB

    @pl.when(pl.program_id(0) == 0)
    def _():
        carry_ref[...] = jnp.zeros_like(carry_ref)

    xn = _rms(x_ref[...], g_ref[...]).astype(BF16)
    gate_ref[...] = _dot(xn, wgate_ref[...])
    u_ref[...] = _dot(xn, wu_ref[...])

    ba = _dot(xn, wba_ref[...])
    sp_in = ba + dtb_ref[...]
    softplus = jnp.maximum(sp_in, 0.0) + jnp.log1p(jnp.exp(-jnp.abs(sp_in)))
    lane = lax.broadcasted_iota(jnp.int32, ba.shape, 1)
    bg_ref[...] = jnp.where(lane < N_HEADS_A, _sigmoid(ba), -jnp.exp(alog_ref[...]) * softplus)

    outs = (q_ref, k_ref, v_ref)
    for s in range(3 * N_HEADS_A):
        sl = slice(s * LANE, (s + 1) * LANE)
        pre = _dot(xn, wqkv_ref[:, sl])
        ext = jnp.concatenate([carry_ref[:, sl], pre], axis=0)
        carry_ref[:, sl] = pre[tm - hist:, :]
        acc = ext[hist:, :] * conv_ref[CONV_A - 1:CONV_A, sl]
        for j in range(1, CONV_A):
            acc = acc + ext[hist - NB * j: hist - NB * j + tm, :] * conv_ref[CONV_A - 1 - j:CONV_A - j, sl]
        y = _silu(acc)
        which, h = divmod(s, N_HEADS_A)
        if which < 2:
            y = y * lax.rsqrt(jnp.sum(y * y, axis=-1, keepdims=True) + 1e-6)
        if which == 0:
            y = y * (HEAD_DIM_A ** -0.5)
        outs[which][h] = y


def _mix_in(x2, g, wqkv, wgate, wba, wu, conv, alog, dtb, tt):
    rows = x2.shape[0]
    tm = tt * NB
    head = jax.ShapeDtypeStruct((N_HEADS_A, rows, HEAD_DIM_A), F32)
    head_spec = pl.BlockSpec((N_HEADS_A, tm, HEAD_DIM_A), lambda i: (0, i, 0))
    row_spec = lambda w: pl.BlockSpec((tm, w), lambda i: (i, 0))
    return pl.pallas_call(
        _mix_in_kernel,
        grid=(rows // tm,),
        in_specs=[row_spec(D_MODEL), _full((1, D_MODEL)), _full(wqkv.shape), _full(wgate.shape),
                  _full(wba.shape), _full(wu.shape), _full(conv.shape), _full((1, LANE)), _full((1, LANE))],
        out_specs=[head_spec, head_spec, head_spec, row_spec(LANE), row_spec(WIDTH_A), row_spec(SSM_WIDTH)],
        out_shape=[head, head, head,
                   jax.ShapeDtypeStruct((rows, LANE), F32),
                   jax.ShapeDtypeStruct((rows, WIDTH_A), F32),
                   jax.ShapeDtypeStruct((rows, SSM_WIDTH), F32)],
        scratch_shapes=[pltpu.VMEM(((CONV_A - 1) * NB, 3 * WIDTH_A), F32)],
        compiler_params=_params(1),
        name="mix_in",
    )(x2, g, wqkv, wgate, wba, wu, conv, alog, dtb)


def _split3(x):
    hi = x.astype(BF16).astype(F32)
    mid = (x - hi).astype(BF16).astype(F32)
    lo = (x - hi - mid).astype(BF16).astype(F32)
    return hi, mid, lo


def _delta_kernel(q_ref, k_ref, v_ref, bg_ref, o_ref, s_ref):
    h = pl.program_id(0)
    c2 = 2 * CHUNK

    @pl.when(pl.program_id(1) == 0)
    def _():
        s_ref[...] = jnp.zeros_like(s_ref)

    row = lax.broadcasted_iota(jnp.int32, (c2, c2), 0)
    col = lax.broadcasted_iota(jnp.int32, (c2, c2), 1)
    same = (row >= CHUNK) == (col >= CHUNK)
    causal = same & (row >= col)
    strict = same & (row > col)
    tri = jnp.where(causal, 1.0, 0.0).astype(BF16)
    eye = jnp.where(row == col, 1.0, 0.0)
    lane = lax.broadcasted_iota(jnp.int32, (c2, LANE), 1)
    top = lax.broadcasted_iota(jnp.int32, (c2, LANE), 0) < CHUNK

    def stacked(ref, p):
        return jnp.concatenate([ref[pl.ds(2 * p, CHUNK, stride=NB), :],
                                ref[pl.ds(2 * p + 1, CHUNK, stride=NB), :]], axis=0)

    for p in range(NB // 2):
        q2, k2, v2, bg2 = stacked(q_ref, p), stacked(k_ref, p), stacked(v_ref, p), stacked(bg_ref, p)
        beta = jnp.sum(jnp.where(lane == h, bg2, 0.0), axis=-1, keepdims=True)
        g = jnp.sum(jnp.where(lane == h + N_HEADS_A, bg2, 0.0), axis=-1, keepdims=True)
        ghi, gmid, glo = _split3(g)
        g3 = jnp.where(lane == 0, ghi, jnp.where(lane == 1, gmid, jnp.where(lane == 2, glo, 0.0)))
        gc = jnp.sum(jnp.dot(tri, g3.astype(BF16), preferred_element_type=F32), axis=-1, keepdims=True)
        gcb = jnp.broadcast_to(gc, (c2, c2))
        diff = gcb - gcb.T
        decay = jnp.where(causal, jnp.exp(jnp.where(causal, diff, 0.0)), 0.0)
        eg = jnp.exp(gc)

        kb = k2.astype(BF16)
        a_mat = jnp.where(strict, _dot_nt(kb, kb) * decay * beta, 0.0)
        qk = jnp.where(causal, _dot_nt(q2, kb) * decay, 0.0)
        pw = -a_mat
        tinv = eye + pw
        for _ in range(int(math.log2(CHUNK)) - 1):
            pw = _dot(pw, pw)
            tinv = tinv + _dot(tinv, pw)
        rhs = jnp.concatenate([v2 * beta, k2 * (beta * eg)], axis=1)
        sol = _dot(tinv, rhs)
        u2, w2 = sol[:, :HEAD_DIM_A], sol[:, HEAD_DIM_A:]

        s01 = s_ref[p]
        ws = _dot(w2, s01)
        vn = u2 - jnp.where(top, ws[:, :HEAD_DIM_A], ws[:, HEAD_DIM_A:])
        qs = _dot(q2 * eg, s01)
        o2 = jnp.where(top, qs[:, :HEAD_DIM_A], qs[:, HEAD_DIM_A:]) + _dot(qk, vn)
        o_ref[pl.ds(2 * p, CHUNK, stride=NB), :] = o2[:CHUNK]
        o_ref[pl.ds(2 * p + 1, CHUNK, stride=NB), :] = o2[CHUNK:]

        gl0, gl1 = gc[CHUNK - 1:CHUNK, :], gc[c2 - 1:c2, :]
        glast = jnp.where(top[:, :1], gl0, gl1)
        kd = k2 * jnp.exp(glast - gc)
        vn01 = jnp.concatenate([jnp.where(top, vn, 0.0), jnp.where(top, 0.0, vn)], axis=1)
        lane2 = lax.broadcasted_iota(jnp.int32, (1, 2 * HEAD_DIM_A), 1)
        sdec = jnp.where(lane2 < HEAD_DIM_A, jnp.exp(gl0), jnp.exp(gl1))
        s_ref[p] = s01 * sdec + _dot(kd.T, vn01)


def _delta(q, k, v, bg):
    rows = q.shape[1]
    cm = CHUNK * NB
    head_spec = pl.BlockSpec((None, cm, HEAD_DIM_A), lambda h, c: (h, c, 0))
    return pl.pallas_call(
        _delta_kernel,
        grid=(N_HEADS_A, rows // cm),
        in_specs=[head_spec, head_spec, head_spec, pl.BlockSpec((cm, LANE), lambda h, c: (c, 0))],
        out_specs=head_spec,
        out_shape=jax.ShapeDtypeStruct(q.shape, F32),
        scratch_shapes=[pltpu.VMEM((NB // 2, HEAD_DIM_A, 2 * HEAD_DIM_A), F32)],
        compiler_params=_params(2),
        name="delta_rule",
    )(q, k, v, bg)


def _gelu_tanh(x):
    return 0.5 * x * (1.0 + jnp.tanh(math.sqrt(2.0 / math.pi) * (x + 0.044715 * (x * x * x))))


def _s5_out_kernel(x_ref, u_ref, o_ref, gate_ref, bmat_ref, lre_ref, lim_ref, cmat_ref, d_ref,
                   wglu_ref, bglu_ref, onorm_ref, wout_ref, out_ref, st_ref, xs_ref):
    tm = x_ref.shape[0]
    tt = tm // NB
    ns = SUPER_STATE

    @pl.when(pl.program_id(0) == 0)
    def _():
        st_ref[...] = jnp.zeros_like(st_ref)

    u = u_ref[...]
    ys = []
    for j in range(N_SUPER):
        xs_ref[...] = _dot(u[:, j * LANE:(j + 1) * LANE], bmat_ref[j])
        a_re = jnp.broadcast_to(lre_ref[:, j * ns:(j + 1) * ns], (NB, ns))
        a_im = jnp.broadcast_to(lim_ref[:, j * ns:(j + 1) * ns], (NB, ns))

        def step(t, carry):
            xr, xi = carry
            rows = pl.ds(pl.multiple_of(t * NB, NB), NB)
            nr = a_re * xr - a_im * xi + xs_ref[rows, :ns]
            ni = a_re * xi + a_im * xr + xs_ref[rows, ns:]
            xs_ref[rows, :ns] = nr
            xs_ref[rows, ns:] = ni
            return nr, ni

        xr, xi = lax.fori_loop(0, tt, step, (st_ref[j, :, :ns], st_ref[j, :, ns:]), unroll=8)
        st_ref[j, :, :ns] = xr
        st_ref[j, :, ns:] = xi
        ys.append(_dot(xs_ref[...], cmat_ref[j]))
    y = jnp.concatenate(ys, axis=1) + d_ref[...] * u
    y = _gelu_tanh(y)
    yb = y * _sigmoid(_dot(y, wglu_ref[...]) + bglu_ref[...])

    gate = gate_ref[...]
    parts = []
    for h in range(N_HEADS_A):
        parts.append(_rms(o_ref[h], onorm_ref[...]) * _silu(gate[:, h * HEAD_DIM_A:(h + 1) * HEAD_DIM_A]))
    mixed = jnp.concatenate(parts + [yb], axis=1)
    out_ref[...] = x_ref[...] + _dot(mixed, wout_ref[...])


def _s5_out(x2, u, o, gate, bmat, lre, lim, cmat, dvec, wglu, bglu, onorm, wout, tt):
    rows = x2.shape[0]
    tm = tt * NB
    row_spec = lambda w: pl.BlockSpec((tm, w), lambda i: (i, 0))
    return pl.pallas_call(
        _s5_out_kernel,
        grid=(rows // tm,),
        in_specs=[row_spec(D_MODEL), row_spec(SSM_WIDTH),
                  pl.BlockSpec((N_HEADS_A, tm, HEAD_DIM_A), lambda i: (0, i, 0)), row_spec(WIDTH_A),
                  _full(bmat.shape), _full(lre.shape), _full(lim.shape), _full(cmat.shape),
                  _full(dvec.shape), _full(wglu.shape), _full(bglu.shape), _full(onorm.shape),
                  _full(wout.shape)],
        out_specs=row_spec(D_MODEL),
        out_shape=jax.ShapeDtypeStruct(x2.shape, F32),
        scratch_shapes=[pltpu.VMEM((N_SUPER, NB, 2 * SUPER_STATE), F32),
                        pltpu.VMEM((tm, 2 * SUPER_STATE), F32)],
        compiler_params=_params(1),
        name="s5_mix_out",
    )(x2, u, o, gate, bmat, lre, lim, cmat, dvec, wglu, bglu, onorm, wout)


def _pool_kernel(x_ref, g_ref, pw_ref, ps_ref, out_ref, carry_ref):
    tm = x_ref.shape[0]
    tt = tm // NB
    hist = carry_ref.shape[0]

    @pl.when(pl.program_id(0) == 0)
    def _():
        carry_ref[...] = jnp.zeros_like(carry_ref)

    x = x_ref[...]
    xn = _rms(x, g_ref[...])
    tpos = pl.program_id(0) * tt + lax.broadcasted_iota(jnp.int32, (tm, 1), 0) // NB + 1
    for gi, win in enumerate(POOL_WINDOWS):
        sl = slice(gi * POOL_GROUP, (gi + 1) * POOL_GROUP)
        xg = xn[:, sl]
        s = jnp.concatenate([carry_ref[:, sl], xg], axis=0)
        span = 1
        while span < win:
            s = s[NB * span:, :] + s[:s.shape[0] - NB * span, :]
            span *= 2
        s = s[s.shape[0] - tm:, :]
        count = jnp.minimum(tpos, win).astype(F32)
        mean = s * (1.0 / count)
        y = _dot(mean - xg, pw_ref[gi]) * ps_ref[:, sl]
        out_ref[:, sl] = x[:, sl] + y
    carry_ref[...] = xn[tm - hist:, :]


def _pool(x2, g, pw, ps, tt):
    rows = x2.shape[0]
    tm = tt * NB
    row_spec = pl.BlockSpec((tm, D_MODEL), lambda i: (i, 0))
    return pl.pallas_call(
        _pool_kernel,
        grid=(rows // tm,),
        in_specs=[row_spec, _full((1, D_MODEL)), _full(pw.shape), _full((1, D_MODEL))],
        out_specs=row_spec,
        out_shape=jax.ShapeDtypeStruct(x2.shape, F32),
        scratch_shapes=[pltpu.VMEM((POOL_WINDOWS[-1] * NB, D_MODEL), F32)],
        compiler_params=_params(1),
        name="pool_mixer",
    )(x2, g, pw, ps)


def _xattn_kernel(x_ref, g_ref, wq_ref, wo_ref, k_ref, v_ref, out_ref, qs_ref, os_ref):
    tm = x_ref.shape[0]
    tt = tm // NB
    x = x_ref[...]
    q = _dot(_rms(x, g_ref[...]), wq_ref[...]) * (HEAD_DIM_X ** -0.5)
    n_slab = D_MODEL // LANE
    for s in range(n_slab):
        qs_ref[s] = q[:, s * LANE:(s + 1) * LANE]
    per_head = HEAD_DIM_X // LANE
    for b in range(NB):
        rows = pl.ds(b, tt, stride=NB)
        for h in range(N_HEADS_X):
            hs = slice(h * HEAD_DIM_X, (h + 1) * HEAD_DIM_X)
            qbh = jnp.concatenate([qs_ref[per_head * h + i, rows, :] for i in range(per_head)], axis=1)
            sc = _dot_nt(qbh, k_ref[b, :, hs])
            e = jnp.exp(sc - jnp.max(sc, axis=-1, keepdims=True))
            obh = _dot(e, v_ref[b, :, hs]) / jnp.sum(e, axis=-1, keepdims=True)
            for i in range(per_head):
                os_ref[per_head * h + i, rows, :] = obh[:, i * LANE:(i + 1) * LANE]
    o = jnp.concatenate([os_ref[s] for s in range(n_slab)], axis=1)
    out_ref[...] = x + _dot(o, wo_ref[...])


def _xattn(x2, g, wq, wo, k, v, tt):
    rows = x2.shape[0]
    tm = tt * NB
    row_spec = pl.BlockSpec((tm, D_MODEL), lambda i: (i, 0))
    return pl.pallas_call(
        _xattn_kernel,
        grid=(rows // tm,),
        in_specs=[row_spec, _full((1, D_MODEL)), _full(wq.shape), _full(wo.shape),
                  _full(k.shape), _full(v.shape)],
        out_specs=row_spec,
        out_shape=jax.ShapeDtypeStruct(x2.shape, F32),
        scratch_shapes=[pltpu.VMEM((D_MODEL // LANE, tm, LANE), F32),
                        pltpu.VMEM((D_MODEL // LANE, tm, LANE), F32)],
        compiler_params=_params(1),
        name="mem_xattn",
    )(x2, g, wq, wo, k, v)


def _ffn_kernel(x_ref, g_ref, wg_ref, wv_ref, cg_ref, cv_ref, wd_ref, gf_ref, out_ref,
                carry_g_ref, carry_v_ref, acc_ref, *, final_norm):
    tm = x_ref.shape[0]
    hist = (CONV_FFN - 1) * NB

    @pl.when(pl.program_id(0) == 0)
    def _():
        carry_g_ref[...] = jnp.zeros_like(carry_g_ref)
        carry_v_ref[...] = jnp.zeros_like(carry_v_ref)

    x = x_ref[...]
    xn = _rms(x, g_ref[...]).astype(BF16)

    def conv(w_ref, c_ref, carry_ref, sl):
        pre = _dot(xn, w_ref[:, sl])
        ext = jnp.concatenate([carry_ref[:, sl], pre], axis=0)
        carry_ref[:, sl] = pre[tm - hist:, :]
        acc = ext[hist:, :] * c_ref[CONV_FFN - 1:CONV_FFN, sl]
        for j in range(1, CONV_FFN):
            acc = acc + ext[hist - NB * j: hist - NB * j + tm, :] * c_ref[CONV_FFN - 1 - j:CONV_FFN - j, sl]
        return acc

    for c in range(D_FF // FF_CHUNK):
        sl = slice(c * FF_CHUNK, (c + 1) * FF_CHUNK)
        act = _silu(conv(wg_ref, cg_ref, carry_g_ref, sl)) * conv(wv_ref, cv_ref, carry_v_ref, sl)
        part = _dot(act, wd_ref[sl, :])
        if c == 0:
            acc_ref[...] = part
        else:
            acc_ref[...] += part
    y = x + acc_ref[...]
    if final_norm:
        y = _rms(y, gf_ref[...])
    out_ref[...] = y


def _ffn(x2, g, wg, wv, cg, cv, wd, gf, tt, final_norm):
    rows = x2.shape[0]
    tm = tt * NB
    row_spec = pl.BlockSpec((tm, D_MODEL), lambda i: (i, 0))
    hist = (CONV_FFN - 1) * NB
    return pl.pallas_call(
        functools.partial(_ffn_kernel, final_norm=final_norm),
        grid=(rows // tm,),
        in_specs=[row_spec, _full((1, D_MODEL)), _full(wg.shape), _full(wv.shape), _full(cg.shape),
                  _full(cv.shape), _full(wd.shape), _full((1, D_MODEL))],
        out_specs=row_spec,
        out_shape=jax.ShapeDtypeStruct(x2.shape, F32),
        scratch_shapes=[pltpu.VMEM((hist, D_FF), F32), pltpu.VMEM((hist, D_FF), F32),
                        pltpu.VMEM((tm, D_MODEL), F32)],
        compiler_params=_params(1),
        name="conv_ffn",
    )(x2, g, wg, wv, cg, cv, wd, gf)


def _s5_tables(lam_re, lam_im, b_re, b_im, c_re, c_im, log_dt):
    dt = jnp.exp(log_dt)[:, None]
    mag = jnp.exp(lam_re * dt)
    ang = lam_im * dt
    lb_re, lb_im = mag * jnp.cos(ang), mag * jnp.sin(ang)
    den = lam_re * lam_re + lam_im * lam_im
    nr, ni = lb_re - 1.0, lb_im
    coef_re = (nr * lam_re + ni * lam_im) / den
    coef_im = (ni * lam_re - nr * lam_im) / den
    bb_re = coef_re[..., None] * b_re - coef_im[..., None] * b_im
    bb_im = coef_re[..., None] * b_im + coef_im[..., None] * b_re
    eye = jnp.eye(8, dtype=F32)

    def in_blocks(bb):
        bb = bb.reshape(N_SUPER, 8, SSM_STATE, SSM_GROUP)
        return jnp.einsum('jgph,gm->jghmp', bb, eye).reshape(N_SUPER, LANE, SUPER_STATE)

    def out_blocks(cc):
        cc = cc.reshape(N_SUPER, 8, SSM_GROUP, SSM_STATE)
        return jnp.einsum('jghp,gm->jgpmh', cc, eye).reshape(N_SUPER, SUPER_STATE, LANE)

    bmat = jnp.concatenate([in_blocks(bb_re), in_blocks(bb_im)], axis=2).astype(BF16)
    cmat = jnp.concatenate([out_blocks(c_re), -out_blocks(c_im)], axis=1).astype(BF16)
    return bmat, cmat, lb_re.reshape(1, -1), lb_im.reshape(1, -1)


def _pad_lanes(v, offset):
    return jnp.zeros((1, LANE), F32).at[0, offset:offset + v.shape[0]].set(v)


def kernel(x, mem, norm_mix_g, norm_xa_g, norm_ffn_g, norm_mem_g, norm_final_g, w_in_ab, conv_qkv_a, a_log_a, dt_bias_a, onorm_g_a, ssm_lambda_re, ssm_lambda_im, ssm_b_re, ssm_b_im, ssm_c_re, ssm_c_im, ssm_d, ssm_log_dt, w_glu_b, b_glu_b, w_out_ab, pool_w, pool_scale, xa_wq, xa_wkv, xa_wo, ffn_w_up, ffn_conv, ffn_w_down):
    b, t, d = x.shape
    assert b == NB and d == D_MODEL and t % CHUNK == 0
    tt = min(CHUNK, t)
    row = lambda v: v.reshape(1, -1).astype(F32)
    x2 = jnp.transpose(x, (1, 0, 2)).reshape(t * NB, d)

    k_mem, v_mem = _memkv(mem, row(norm_mem_g), xa_wkv.astype(BF16))

    for layer in range(2):
        if layer == 0:
            w_in = w_in_ab[0]
            qkv_w = 3 * WIDTH_A
            wba = jnp.zeros((d, LANE), F32).at[:, :2 * N_HEADS_A].set(
                w_in[:, qkv_w + WIDTH_A: qkv_w + WIDTH_A + 2 * N_HEADS_A])
            q, k, v, bg, gate, u = _mix_in(
                x2, row(norm_mix_g[0]), w_in[:, :qkv_w].astype(BF16),
                w_in[:, qkv_w:qkv_w + WIDTH_A].astype(BF16), wba.astype(BF16),
                w_in[:, qkv_w + WIDTH_A + 2 * N_HEADS_A:].astype(BF16), conv_qkv_a[0],
                _pad_lanes(a_log_a[0], N_HEADS_A), _pad_lanes(dt_bias_a[0], N_HEADS_A), tt)
            o = _delta(q, k, v, bg)
            bmat, cmat, lre, lim = _s5_tables(ssm_lambda_re[0], ssm_lambda_im[0], ssm_b_re[0], ssm_b_im[0],
                                              ssm_c_re[0], ssm_c_im[0], ssm_log_dt[0])
            x2 = _s5_out(x2, u, o, gate, bmat, lre, lim, cmat, row(ssm_d[0]), w_glu_b[0].astype(BF16),
                         row(b_glu_b[0]), row(onorm_g_a[0]), w_out_ab[0].astype(BF16), tt)
        else:
            x2 = _pool(x2, row(norm_mix_g[1]), pool_w[0].astype(BF16), row(pool_scale[0]), tt)
        x2 = _xattn(x2, row(norm_xa_g[layer]), xa_wq[layer].astype(BF16), xa_wo[layer].astype(BF16),
                    k_mem[layer], v_mem[layer], tt)
        x2 = _ffn(x2, row(norm_ffn_g[layer]), ffn_w_up[layer][:, :D_FF].astype(BF16),
                  ffn_w_up[layer][:, D_FF:].astype(BF16), ffn_conv[layer][:, :D_FF], ffn_conv[layer][:, D_FF:],
                  ffn_w_down[layer].astype(BF16), row(norm_final_g), tt, final_norm=(layer == 1))
    return jnp.transpose(x2.reshape(t, NB, d), (1, 0, 2))
```

```python
import functools
import math

import jax
import jax.numpy as jnp
from jax import lax
from jax.experimental import pallas as pl
from jax.experimental.pallas import tpu as pltpu

F32 = jnp.float32
BF16 = jnp.bfloat16

NB = 8
D_MODEL = 1024
LANE = 128
N_HEADS_A = 4
HEAD_DIM_A = 128
WIDTH_A = 512
CONV_A = 4
CHUNK = 64
DELTA_HEADS = 2
SSM_WIDTH = 512
SSM_GROUP = 16
N_GROUPS = 32
SSM_STATE = 64
N_SUPER = 4
SUPER_STATE = 8 * SSM_STATE
POOL_WINDOWS = (2, 4, 8, 16)
POOL_GROUP = 256
MEM_LEN = 256
N_HEADS_X = 4
HEAD_DIM_X = 256
D_FF = 2816
CONV_FFN = 3
FF_CHUNK = 256
RMS_EPS = 1e-6
VMEM_LIMIT = 56 * 1024 * 1024


def _dot(a, b):
    return jnp.dot(a.astype(BF16), b.astype(BF16), preferred_element_type=F32)


def _dot_nt(a, b):
    return lax.dot_general(a.astype(BF16), b.astype(BF16), (((1,), (1,)), ((), ())),
                           preferred_element_type=F32)


def _rms(x, g):
    return x * lax.rsqrt(jnp.mean(x * x, axis=-1, keepdims=True) + RMS_EPS) * g


def _sigmoid(x):
    return 1.0 / (1.0 + jnp.exp(-x))


def _silu(x):
    return x * _sigmoid(x)


def _params(n_axes=1):
    return pltpu.CompilerParams(dimension_semantics=("arbitrary",) * n_axes,
                                vmem_limit_bytes=VMEM_LIMIT)


def _full(shape):
    return pl.BlockSpec(shape, lambda *_: (0,) * len(shape))


def _memkv_kernel(mem_ref, g_ref, wkv_ref, k_ref, v_ref):
    mn = _rms(mem_ref[...], g_ref[...])
    kv = _dot(mn, wkv_ref[...])
    k_ref[...] = kv[:, :D_MODEL].astype(BF16)
    v_ref[...] = kv[:, D_MODEL:].astype(BF16)


def _memkv(mem, g, wkv):
    n_layers = wkv.shape[0]
    out = jax.ShapeDtypeStruct((n_layers, NB, MEM_LEN, D_MODEL), BF16)
    return pl.pallas_call(
        _memkv_kernel,
        grid=(n_layers, NB),
        in_specs=[pl.BlockSpec((None, MEM_LEN, D_MODEL), lambda l, b: (b, 0, 0)),
                  pl.BlockSpec((1, D_MODEL), lambda l, b: (0, 0)),
                  pl.BlockSpec((None, D_MODEL, 2 * D_MODEL), lambda l, b: (l, 0, 0))],
        out_specs=[pl.BlockSpec((None, None, MEM_LEN, D_MODEL), lambda l, b: (l, b, 0, 0))] * 2,
        out_shape=[out, out],
        compiler_params=_params(2),
        name="memkv",
    )(mem, g, wkv)


def _mix_in_kernel(x_ref, g_ref, wqkv_ref, wgate_ref, wba_ref, wu_ref, conv_ref, alog_ref, dtb_ref,
                   q_ref, k_ref, v_ref, bg_ref, gate_ref, u_ref, carry_ref):
    tm = x_ref.shape[0]
    hist = (CONV_A - 1) * NB

    @pl.when(pl.program_id(0) == 0)
    def _():
        carry_ref[...] = jnp.zeros_like(carry_ref)

    xn = _rms(x_ref[...], g_ref[...]).astype(BF16)
    gate_ref[...] = _dot(xn, wgate_ref[...])
    u_ref[...] = _dot(xn, wu_ref[...])

    ba = _dot(xn, wba_ref[...])
    sp_in = ba + dtb_ref[...]
    softplus = jnp.maximum(sp_in, 0.0) + jnp.log1p(jnp.exp(-jnp.abs(sp_in)))
    lane = lax.broadcasted_iota(jnp.int32, ba.shape, 1)
    bg_ref[...] = jnp.where(lane < N_HEADS_A, _sigmoid(ba), -jnp.exp(alog_ref[...]) * softplus)

    outs = (q_ref, k_ref, v_ref)
    for s in range(3 * N_HEADS_A):
        sl = slice(s * LANE, (s + 1) * LANE)
        pre = _dot(xn, wqkv_ref[:, sl])
        ext = jnp.concatenate([carry_ref[:, sl], pre], axis=0)
        carry_ref[:, sl] = pre[tm - hist:, :]
        acc = ext[hist:, :] * conv_ref[CONV_A - 1:CONV_A, sl]
        for j in range(1, CONV_A):
            acc = acc + ext[hist - NB * j: hist - NB * j + tm, :] * conv_ref[CONV_A - 1 - j:CONV_A - j, sl]
        y = _silu(acc)
        which, h = divmod(s, N_HEADS_A)
        if which < 2:
            y = y * lax.rsqrt(jnp.sum(y * y, axis=-1, keepdims=True) + 1e-6)
        if which == 0:
            y = y * (HEAD_DIM_A ** -0.5)
        outs[which][h] = y


def _mix_in(x2, g, wqkv, wgate, wba, wu, conv, alog, dtb, tt):
    rows = x2.shape[0]
    tm = tt * NB
    head = jax.ShapeDtypeStruct((N_HEADS_A, rows, HEAD_DIM_A), F32)
    head_spec = pl.BlockSpec((N_HEADS_A, tm, HEAD_DIM_A), lambda i: (0, i, 0))
    row_spec = lambda w: pl.BlockSpec((tm, w), lambda i: (i, 0))
    return pl.pallas_call(
        _mix_in_kernel,
        grid=(rows // tm,),
        in_specs=[row_spec(D_MODEL), _full((1, D_MODEL)), _full(wqkv.shape), _full(wgate.shape),
                  _full(wba.shape), _full(wu.shape), _full(conv.shape), _full((1, LANE)), _full((1, LANE))],
        out_specs=[head_spec, head_spec, head_spec, row_spec(LANE), row_spec(WIDTH_A), row_spec(SSM_WIDTH)],
        out_shape=[head, head, head,
                   jax.ShapeDtypeStruct((rows, LANE), F32),
                   jax.ShapeDtypeStruct((rows, WIDTH_A), F32),
                   jax.ShapeDtypeStruct((rows, SSM_WIDTH), F32)],
        scratch_shapes=[pltpu.VMEM(((CONV_A - 1) * NB, 3 * WIDTH_A), F32)],
        compiler_params=_params(1),
        name="mix_in",
    )(x2, g, wqkv, wgate, wba, wu, conv, alog, dtb)


def _split3(x):
    hi = x.astype(BF16).astype(F32)
    mid = (x - hi).astype(BF16).astype(F32)
    lo = (x - hi - mid).astype(BF16).astype(F32)
    return hi, mid, lo


def _delta_kernel(q_ref, k_ref, v_ref, bg_ref, o_ref, s_ref):
    c2 = 2 * CHUNK
    n_pairs = NB // 2

    @pl.when(pl.program_id(1) == 0)
    def _():
        s_ref[...] = jnp.zeros_like(s_ref)

    row = lax.broadcasted_iota(jnp.int32, (c2, c2), 0)
    col = lax.broadcasted_iota(jnp.int32, (c2, c2), 1)
    same = (row >= CHUNK) == (col >= CHUNK)
    causal = same & (row >= col)
    strict = same & (row > col)
    tri = jnp.where(causal, 1.0, 0.0).astype(BF16)
    eye = jnp.where(row == col, 1.0, 0.0)
    lane = lax.broadcasted_iota(jnp.int32, (c2, LANE), 1)
    top = lax.broadcasted_iota(jnp.int32, (c2, LANE), 0) < CHUNK
    lane2 = lax.broadcasted_iota(jnp.int32, (1, 2 * HEAD_DIM_A), 1)

    def stacked(ref, p):
        return jnp.concatenate([ref[pl.ds(2 * p, CHUNK, stride=NB), :],
                                ref[pl.ds(2 * p + 1, CHUNK, stride=NB), :]], axis=0)

    chains = [(hh, p) for hh in range(DELTA_HEADS) for p in range(n_pairs)]
    n = len(chains)
    bg2 = [stacked(bg_ref, p) for p in range(n_pairs)]
    q2 = [stacked(q_ref.at[hh], p) for hh, p in chains]
    k2 = [stacked(k_ref.at[hh], p) for hh, p in chains]
    v2 = [stacked(v_ref.at[hh], p) for hh, p in chains]
    kb = [x.astype(BF16) for x in k2]
    kk = [_dot_nt(kb[i], kb[i]) for i in range(n)]
    qk = [_dot_nt(q2[i], kb[i]) for i in range(n)]

    beta, gc = [], []
    for hh, p in chains:
        h = pl.program_id(0) * DELTA_HEADS + hh
        beta.append(jnp.sum(jnp.where(lane == h, bg2[p], 0.0), axis=-1, keepdims=True))
        g = jnp.sum(jnp.where(lane == h + N_HEADS_A, bg2[p], 0.0), axis=-1, keepdims=True)
        ghi, gmid, glo = _split3(g)
        g3 = jnp.where(lane == 0, ghi, jnp.where(lane == 1, gmid, jnp.where(lane == 2, glo, 0.0)))
        gc.append(jnp.sum(jnp.dot(tri, g3.astype(BF16), preferred_element_type=F32), axis=-1, keepdims=True))

    pw, tinv, eg = [], [], []
    for i in range(n):
        gcb = jnp.broadcast_to(gc[i], (c2, c2))
        decay = jnp.where(causal, jnp.exp(jnp.where(causal, gcb - gcb.T, 0.0)), 0.0)
        eg.append(jnp.exp(gc[i]))
        pw.append(jnp.where(strict, kk[i] * decay * (-beta[i]), 0.0))
        qk[i] = jnp.where(causal, qk[i] * decay, 0.0)
        tinv.append(eye + pw[i])
    for _ in range(int(math.log2(CHUNK)) - 1):
        pw = [_dot(pw[i], pw[i]) for i in range(n)]
        tinv = [tinv[i] + _dot(tinv[i], pw[i]) for i in range(n)]
    sol = [_dot(tinv[i], jnp.concatenate([v2[i] * beta[i], k2[i] * (beta[i] * eg[i])], axis=1))
           for i in range(n)]

    s01 = [s_ref[i] for i in range(n)]
    ws = [_dot(sol[i][:, HEAD_DIM_A:], s01[i]) for i in range(n)]
    qs = [_dot(q2[i] * eg[i], s01[i]) for i in range(n)]
    vn = [sol[i][:, :HEAD_DIM_A] - jnp.where(top, ws[i][:, :HEAD_DIM_A], ws[i][:, HEAD_DIM_A:]) for i in range(n)]
    o2 = [jnp.where(top, qs[i][:, :HEAD_DIM_A], qs[i][:, HEAD_DIM_A:]) + _dot(qk[i], vn[i]) for i in range(n)]
    for i, (hh, p) in enumerate(chains):
        o_ref[hh, pl.ds(2 * p, CHUNK, stride=NB), :] = o2[i][:CHUNK]
        o_ref[hh, pl.ds(2 * p + 1, CHUNK, stride=NB), :] = o2[i][CHUNK:]
    for i in range(n):
        gl0, gl1 = gc[i][CHUNK - 1:CHUNK, :], gc[i][c2 - 1:c2, :]
        kd = k2[i] * jnp.exp(jnp.where(top[:, :1], gl0, gl1) - gc[i])
        vn01 = jnp.concatenate([jnp.where(top, vn[i], 0.0), jnp.where(top, 0.0, vn[i])], axis=1)
        sdec = jnp.where(lane2 < HEAD_DIM_A, jnp.exp(gl0), jnp.exp(gl1))
        s_ref[i] = s01[i] * sdec + _dot(kd.T, vn01)


def _delta(q, k, v, bg):
    rows = q.shape[1]
    cm = CHUNK * NB
    head_spec = pl.BlockSpec((DELTA_HEADS, cm, HEAD_DIM_A), lambda h, c: (h, c, 0))
    return pl.pallas_call(
        _delta_kernel,
        grid=(N_HEADS_A // DELTA_HEADS, rows // cm),
        in_specs=[head_spec, head_spec, head_spec, pl.BlockSpec((cm, LANE), lambda h, c: (c, 0))],
        out_specs=head_spec,
        out_shape=jax.ShapeDtypeStruct(q.shape, F32),
        scratch_shapes=[pltpu.VMEM((DELTA_HEADS * NB // 2, HEAD_DIM_A, 2 * HEAD_DIM_A), F32)],
        compiler_params=_params(2),
        name="delta_rule",
    )(q, k, v, bg)


def _gelu_tanh(x):
    return 0.5 * x * (1.0 + jnp.tanh(math.sqrt(2.0 / math.pi) * (x + 0.044715 * (x * x * x))))


def _s5_out_kernel(x_ref, u_ref, o_ref, gate_ref, bmat_ref, lre_ref, lim_ref, cmat_ref, d_ref,
                   wglu_ref, bglu_ref, onorm_ref, wout_ref, out_ref, st_ref, xs_ref):
    tm = x_ref.shape[0]
    tt = tm // NB
    ns = SUPER_STATE

    @pl.when(pl.program_id(0) == 0)
    def _():
        st_ref[...] = jnp.zeros_like(st_ref)

    u = u_ref[...]
    ys = []
    for j in range(N_SUPER):
        xs_ref[...] = _dot(u[:, j * LANE:(j + 1) * LANE], bmat_ref[j])
        a_re = jnp.broadcast_to(lre_ref[:, j * ns:(j + 1) * ns], (NB, ns))
        a_im = jnp.broadcast_to(lim_ref[:, j * ns:(j + 1) * ns], (NB, ns))

        def step(t, carry):
            xr, xi = carry
            rows = pl.ds(pl.multiple_of(t * NB, NB), NB)
            nr = a_re * xr - a_im * xi + xs_ref[rows, :ns]
            ni = a_re * xi + a_im * xr + xs_ref[rows, ns:]
            xs_ref[rows, :ns] = nr
            xs_ref[rows, ns:] = ni
            return nr, ni

        xr, xi = lax.fori_loop(0, tt, step, (st_ref[j, :, :ns], st_ref[j, :, ns:]), unroll=8)
        st_ref[j, :, :ns] = xr
        st_ref[j, :, ns:] = xi
        ys.append(_dot(xs_ref[...], cmat_ref[j]))
    y = jnp.concatenate(ys, axis=1) + d_ref[...] * u
    y = _gelu_tanh(y)
    yb = y * _sigmoid(_dot(y, wglu_ref[...]) + bglu_ref[...])

    gate = gate_ref[...]
    parts = []
    for h in range(N_HEADS_A):
        parts.append(_rms(o_ref[h], onorm_ref[...]) * _silu(gate[:, h * HEAD_DIM_A:(h + 1) * HEAD_DIM_A]))
    mixed = jnp.concatenate(parts + [yb], axis=1)
    out_ref[...] = x_ref[...] + _dot(mixed, wout_ref[...])


def _s5_out(x2, u, o, gate, bmat, lre, lim, cmat, dvec, wglu, bglu, onorm, wout, tt):
    rows = x2.shape[0]
    tm = tt * NB
    row_spec = lambda w: pl.BlockSpec((tm, w), lambda i: (i, 0))
    return pl.pallas_call(
        _s5_out_kernel,
        grid=(rows // tm,),
        in_specs=[row_spec(D_MODEL), row_spec(SSM_WIDTH),
                  pl.BlockSpec((N_HEADS_A, tm, HEAD_DIM_A), lambda i: (0, i, 0)), row_spec(WIDTH_A),
                  _full(bmat.shape), _full(lre.shape), _full(lim.shape), _full(cmat.shape),
                  _full(dvec.shape), _full(wglu.shape), _full(bglu.shape), _full(onorm.shape),
                  _full(wout.shape)],
        out_specs=row_spec(D_MODEL),
        out_shape=jax.ShapeDtypeStruct(x2.shape, F32),
        scratch_shapes=[pltpu.VMEM((N_SUPER, NB, 2 * SUPER_STATE), F32),
                        pltpu.VMEM((tm, 2 * SUPER_STATE), F32)],
        compiler_params=_params(1),
        name="s5_mix_out",
    )(x2, u, o, gate, bmat, lre, lim, cmat, dvec, wglu, bglu, onorm, wout)


def _pool_kernel(x_ref, g_ref, pw_ref, ps_ref, out_ref, carry_ref):
    tm = x_ref.shape[0]
    tt = tm // NB
    hist = carry_ref.shape[0]

    @pl.when(pl.program_id(0) == 0)
    def _():
        carry_ref[...] = jnp.zeros_like(carry_ref)

    x = x_ref[...]
    xn = _rms(x, g_ref[...])
    tpos = pl.program_id(0) * tt + lax.broadcasted_iota(jnp.int32, (tm, 1), 0) // NB + 1
    for gi, win in enumerate(POOL_WINDOWS):
        sl = slice(gi * POOL_GROUP, (gi + 1) * POOL_GROUP)
        xg = xn[:, sl]
        s = jnp.concatenate([carry_ref[:, sl], xg], axis=0)
        span = 1
        while span < win:
            s = s[NB * span:, :] + s[:s.shape[0] - NB * span, :]
            span *= 2
        s = s[s.shape[0] - tm:, :]
        count = jnp.minimum(tpos, win).astype(F32)
        mean = s * (1.0 / count)
        y = _dot(mean - xg, pw_ref[gi]) * ps_ref[:, sl]
        out_ref[:, sl] = x[:, sl] + y
    carry_ref[...] = xn[tm - hist:, :]


def _pool(x2, g, pw, ps, tt):
    rows = x2.shape[0]
    tm = tt * NB
    row_spec = pl.BlockSpec((tm, D_MODEL), lambda i: (i, 0))
    return pl.pallas_call(
        _pool_kernel,
        grid=(rows // tm,),
        in_specs=[row_spec, _full((1, D_MODEL)), _full(pw.shape), _full((1, D_MODEL))],
        out_specs=row_spec,
        out_shape=jax.ShapeDtypeStruct(x2.shape, F32),
        scratch_shapes=[pltpu.VMEM((POOL_WINDOWS[-1] * NB, D_MODEL), F32)],
        compiler_params=_params(1),
        name="pool_mixer",
    )(x2, g, pw, ps)


def _xattn_kernel(x_ref, g_ref, wq_ref, wo_ref, k_ref, v_ref, out_ref, qs_ref, os_ref):
    tm = x_ref.shape[0]
    tt = tm // NB
    x = x_ref[...]
    q = _dot(_rms(x, g_ref[...]), wq_ref[...]) * (HEAD_DIM_X ** -0.5)
    n_slab = D_MODEL // LANE
    for s in range(n_slab):
        qs_ref[s] = q[:, s * LANE:(s + 1) * LANE]
    per_head = HEAD_DIM_X // LANE
    heads = [slice(h * HEAD_DIM_X, (h + 1) * HEAD_DIM_X) for h in range(N_HEADS_X)]

    def scores(b):
        rows = pl.ds(b, tt, stride=NB)
        out = []
        for h in range(N_HEADS_X):
            qbh = jnp.concatenate([qs_ref[per_head * h + i, rows, :] for i in range(per_head)], axis=1)
            out.append(_dot_nt(qbh, k_ref[b, :, heads[h]]))
        return out

    sc_next = scores(0)
    for b in range(NB):
        sc = sc_next
        if b + 1 < NB:
            sc_next = scores(b + 1)
        e = [jnp.exp(s - jnp.max(s, axis=-1, keepdims=True)) for s in sc]
        rows = pl.ds(b, tt, stride=NB)
        for h in range(N_HEADS_X):
            obh = _dot(e[h], v_ref[b, :, heads[h]]) / jnp.sum(e[h], axis=-1, keepdims=True)
            for i in range(per_head):
                os_ref[per_head * h + i, rows, :] = obh[:, i * LANE:(i + 1) * LANE]
    o = jnp.concatenate([os_ref[s] for s in range(n_slab)], axis=1)
    out_ref[...] = x + _dot(o, wo_ref[...])


def _xattn(x2, g, wq, wo, k, v, tt):
    rows = x2.shape[0]
    tm = tt * NB
    row_spec = pl.BlockSpec((tm, D_MODEL), lambda i: (i, 0))
    return pl.pallas_call(
        _xattn_kernel,
        grid=(rows // tm,),
        in_specs=[row_spec, _full((1, D_MODEL)), _full(wq.shape), _full(wo.shape),
                  _full(k.shape), _full(v.shape)],
        out_specs=row_spec,
        out_shape=jax.ShapeDtypeStruct(x2.shape, F32),
        scratch_shapes=[pltpu.VMEM((D_MODEL // LANE, tm, LANE), F32),
                        pltpu.VMEM((D_MODEL // LANE, tm, LANE), F32)],
        compiler_params=_params(1),
        name="mem_xattn",
    )(x2, g, wq, wo, k, v)


def _ffn_kernel(x_ref, g_ref, wg_ref, wv_ref, cg_ref, cv_ref, wd_ref, gf_ref, out_ref,
                carry_g_ref, carry_v_ref, acc_ref, *, final_norm):
    tm = x_ref.shape[0]
    hist = (CONV_FFN - 1) * NB

    @pl.when(pl.program_id(0) == 0)
    def _():
        carry_g_ref[...] = jnp.zeros_like(carry_g_ref)
        carry_v_ref[...] = jnp.zeros_like(carry_v_ref)

    x = x_ref[...]
    xn = _rms(x, g_ref[...]).astype(BF16)

    def conv(pre, c_ref, carry_ref, sl):
        ext = jnp.concatenate([carry_ref[:, sl], pre], axis=0)
        carry_ref[:, sl] = pre[tm - hist:, :]
        acc = ext[hist:, :] * c_ref[CONV_FFN - 1:CONV_FFN, sl]
        for j in range(1, CONV_FFN):
            acc = acc + ext[hist - NB * j: hist - NB * j + tm, :] * c_ref[CONV_FFN - 1 - j:CONV_FFN - j, sl]
        return acc

    def up(c):
        sl = slice(c * FF_CHUNK, (c + 1) * FF_CHUNK)
        return _dot(xn, wg_ref[:, sl]), _dot(xn, wv_ref[:, sl])

    n_chunks = D_FF // FF_CHUNK
    pre_next = up(0)
    for c in range(n_chunks):
        sl = slice(c * FF_CHUNK, (c + 1) * FF_CHUNK)
        pre_g, pre_v = pre_next
        if c + 1 < n_chunks:
            pre_next = up(c + 1)
        act = _silu(conv(pre_g, cg_ref, carry_g_ref, sl)) * conv(pre_v, cv_ref, carry_v_ref, sl)
        part = _dot(act, wd_ref[sl, :])
        if c == 0:
            acc_ref[...] = part
        else:
            acc_ref[...] += part
    y = x + acc_ref[...]
    if final_norm:
        y = _rms(y, gf_ref[...])
    out_ref[...] = y


def _ffn(x2, g, wg, wv, cg, cv, wd, gf, tt, final_norm):
    rows = x2.shape[0]
    tm = tt * NB
    row_spec = pl.BlockSpec((tm, D_MODEL), lambda i: (i, 0))
    hist = (CONV_FFN - 1) * NB
    return pl.pallas_call(
        functools.partial(_ffn_kernel, final_norm=final_norm),
        grid=(rows // tm,),
        in_specs=[row_spec, _full((1, D_MODEL)), _full(wg.shape), _full(wv.shape), _full(cg.shape),
                  _full(cv.shape), _full(wd.shape), _full((1, D_MODEL))],
        out_specs=row_spec,
        out_shape=jax.ShapeDtypeStruct(x2.shape, F32),
        scratch_shapes=[pltpu.VMEM((hist, D_FF), F32), pltpu.VMEM((hist, D_FF), F32),
                        pltpu.VMEM((tm, D_MODEL), F32)],
        compiler_params=_params(1),
        name="conv_ffn",
    )(x2, g, wg, wv, cg, cv, wd, gf)


def _s5_tables(lam_re, lam_im, b_re, b_im, c_re, c_im, log_dt):
    dt = jnp.exp(log_dt)[:, None]
    mag = jnp.exp(lam_re * dt)
    ang = lam_im * dt
    lb_re, lb_im = mag * jnp.cos(ang), mag * jnp.sin(ang)
    den = lam_re * lam_re + lam_im * lam_im
    nr, ni = lb_re - 1.0, lb_im
    coef_re = (nr * lam_re + ni * lam_im) / den
    coef_im = (ni * lam_re - nr * lam_im) / den
    bb_re = coef_re[..., None] * b_re - coef_im[..., None] * b_im
    bb_im = coef_re[..., None] * b_im + coef_im[..., None] * b_re
    eye = jnp.eye(8, dtype=F32)

    def in_blocks(bb):
        bb = bb.reshape(N_SUPER, 8, SSM_STATE, SSM_GROUP)
        return jnp.einsum('jgph,gm->jghmp', bb, eye).reshape(N_SUPER, LANE, SUPER_STATE)

    def out_blocks(cc):
        cc = cc.reshape(N_SUPER, 8, SSM_GROUP, SSM_STATE)
        return jnp.einsum('jghp,gm->jgpmh', cc, eye).reshape(N_SUPER, SUPER_STATE, LANE)

    bmat = jnp.concatenate([in_blocks(bb_re), in_blocks(bb_im)], axis=2).astype(BF16)
    cmat = jnp.concatenate([out_blocks(c_re), -out_blocks(c_im)], axis=1).astype(BF16)
    return bmat, cmat, lb_re.reshape(1, -1), lb_im.reshape(1, -1)


def _pad_lanes(v, offset):
    return jnp.zeros((1, LANE), F32).at[0, offset:offset + v.shape[0]].set(v)


def kernel(x, mem, norm_mix_g, norm_xa_g, norm_ffn_g, norm_mem_g, norm_final_g, w_in_ab, conv_qkv_a, a_log_a, dt_bias_a, onorm_g_a, ssm_lambda_re, ssm_lambda_im, ssm_b_re, ssm_b_im, ssm_c_re, ssm_c_im, ssm_d, ssm_log_dt, w_glu_b, b_glu_b, w_out_ab, pool_w, pool_scale, xa_wq, xa_wkv, xa_wo, ffn_w_up, ffn_conv, ffn_w_down):
    b, t, d = x.shape
    assert b == NB and d == D_MODEL and t % CHUNK == 0
    tt = min(CHUNK, t)
    row = lambda v: v.reshape(1, -1).astype(F32)
    x2 = jnp.transpose(x, (1, 0, 2)).reshape(t * NB, d)

    k_mem, v_mem = _memkv(mem, row(norm_mem_g), xa_wkv.astype(BF16))

    for layer in range(2):
        if layer == 0:
            w_in = w_in_ab[0]
            qkv_w = 3 * WIDTH_A
            wba = jnp.zeros((d, LANE), F32).at[:, :2 * N_HEADS_A].set(
                w_in[:, qkv_w + WIDTH_A: qkv_w + WIDTH_A + 2 * N_HEADS_A])
            q, k, v, bg, gate, u = _mix_in(
                x2, row(norm_mix_g[0]), w_in[:, :qkv_w].astype(BF16),
                w_in[:, qkv_w:qkv_w + WIDTH_A].astype(BF16), wba.astype(BF16),
                w_in[:, qkv_w + WIDTH_A + 2 * N_HEADS_A:].astype(BF16), conv_qkv_a[0],
                _pad_lanes(a_log_a[0], N_HEADS_A), _pad_lanes(dt_bias_a[0], N_HEADS_A), tt)
            o = _delta(q, k, v, bg)
            bmat, cmat, lre, lim = _s5_tables(ssm_lambda_re[0], ssm_lambda_im[0], ssm_b_re[0], ssm_b_im[0],
                                              ssm_c_re[0], ssm_c_im[0], ssm_log_dt[0])
            x2 = _s5_out(x2, u, o, gate, bmat, lre, lim, cmat, row(ssm_d[0]), w_glu_b[0].astype(BF16),
                         row(b_glu_b[0]), row(onorm_g_a[0]), w_out_ab[0].astype(BF16), tt)
        else:
            x2 = _pool(x2, row(norm_mix_g[1]), pool_w[0].astype(BF16), row(pool_scale[0]), tt)
        x2 = _xattn(x2, row(norm_xa_g[layer]), xa_wq[layer].astype(BF16), xa_wo[layer].astype(BF16),
                    k_mem[layer], v_mem[layer], tt)
        x2 = _ffn(x2, row(norm_ffn_g[layer]), ffn_w_up[layer][:, :D_FF].astype(BF16),
                  ffn_w_up[layer][:, D_FF:].astype(BF16), ffn_conv[layer][:, :D_FF], ffn_conv[layer][:, D_FF:],
                  ffn_w_down[layer].astype(BF16), row(norm_final_g), tt, final_norm=(layer == 1))
    return jnp.transpose(x2.reshape(t, NB, d), (1, 0, 2))
```

```python
import functools
import math

import jax
import jax.numpy as jnp
from jax import lax
from jax.experimental import pallas as pl
from jax.experimental.pallas import tpu as pltpu

F32 = jnp.float32
BF16 = jnp.bfloat16

NB = 8
D_MODEL = 1024
LANE = 128
N_HEADS_A = 4
HEAD_DIM_A = 128
WIDTH_A = 512
CONV_A = 4
CHUNK = 64
DELTA_HEADS = 2
SSM_WIDTH = 512
SSM_GROUP = 16
N_GROUPS = 32
SSM_STATE = 64
N_SUPER = 4
SUPER_STATE = 8 * SSM_STATE
POOL_WINDOWS = (2, 4, 8, 16)
POOL_GROUP = 256
MEM_LEN = 256
N_HEADS_X = 4
HEAD_DIM_X = 256
XATTN_TT = 128
D_FF = 2816
CONV_FFN = 3
FF_CHUNK = 256
RMS_EPS = 1e-6
VMEM_LIMIT = 56 * 1024 * 1024


def _dot(a, b):
    return jnp.dot(a.astype(BF16), b.astype(BF16), preferred_element_type=F32)


def _dot_nt(a, b):
    return lax.dot_general(a.astype(BF16), b.astype(BF16), (((1,), (1,)), ((), ())),
                           preferred_element_type=F32)


def _rms(x, g):
    return x * lax.rsqrt(jnp.mean(x * x, axis=-1, keepdims=True) + RMS_EPS) * g


def _sigmoid(x):
    return 1.0 / (1.0 + jnp.exp(-x))


def _silu(x):
    return x * _sigmoid(x)


def _params(n_axes=1):
    return pltpu.CompilerParams(dimension_semantics=("arbitrary",) * n_axes,
                                vmem_limit_bytes=VMEM_LIMIT)


def _full(shape):
    return pl.BlockSpec(shape, lambda *_: (0,) * len(shape))


def _const(shape):
    return pl.BlockSpec(shape, lambda *_: (0,) * len(shape), pipeline_mode=pl.Buffered(1))


def _memkv_kernel(mem_ref, g_ref, wkv_ref, k_ref, v_ref):
    mn = _rms(mem_ref[...], g_ref[...])
    kv = _dot(mn, wkv_ref[...])
    k_ref[...] = kv[:, :D_MODEL].astype(BF16)
    v_ref[...] = kv[:, D_MODEL:].astype(BF16)


def _memkv(mem, g, wkv):
    n_layers = wkv.shape[0]
    out = jax.ShapeDtypeStruct((n_layers, NB, MEM_LEN, D_MODEL), BF16)
    return pl.pallas_call(
        _memkv_kernel,
        grid=(n_layers, NB),
        in_specs=[pl.BlockSpec((None, MEM_LEN, D_MODEL), lambda l, b: (b, 0, 0)),
                  pl.BlockSpec((1, D_MODEL), lambda l, b: (0, 0)),
                  pl.BlockSpec((None, D_MODEL, 2 * D_MODEL), lambda l, b: (l, 0, 0))],
        out_specs=[pl.BlockSpec((None, None, MEM_LEN, D_MODEL), lambda l, b: (l, b, 0, 0))] * 2,
        out_shape=[out, out],
        compiler_params=_params(2),
        name="memkv",
    )(mem, g, wkv)


def _mix_in_kernel(x_ref, g_ref, wqkv_ref, wgate_ref, wba_ref, wu_ref, conv_ref, alog_ref, dtb_ref,
                   x2_ref, q_ref, k_ref, v_ref, bg_ref, gate_ref, u_ref, carry_ref, xs_ref):
    tt = x_ref.shape[1]
    tm = tt * NB
    hist = (CONV_A - 1) * NB
    wide = 2 * LANE

    @pl.when(pl.program_id(0) == 0)
    def _():
        carry_ref[...] = jnp.zeros_like(carry_ref)

    for b in range(NB):
        for s in range(D_MODEL // LANE):
            xs_ref[s, pl.ds(b, tt, stride=NB), :] = x_ref[b, :, s * LANE:(s + 1) * LANE]
    x = jnp.concatenate([xs_ref[s] for s in range(D_MODEL // LANE)], axis=1)
    x2_ref[...] = x
    xn = _rms(x, g_ref[...]).astype(BF16)

    def pre_qkv(c):
        return _dot(xn, wqkv_ref[:, c * wide:(c + 1) * wide])

    outs = (q_ref, k_ref, v_ref)

    def post_qkv(c, pre):
        sl = slice(c * wide, (c + 1) * wide)
        ext = jnp.concatenate([carry_ref[:, sl], pre], axis=0)
        carry_ref[:, sl] = pre[tm - hist:, :]
        acc = ext[hist:, :] * conv_ref[CONV_A - 1:CONV_A, sl]
        for j in range(1, CONV_A):
            acc = acc + ext[hist - NB * j: hist - NB * j + tm, :] * conv_ref[CONV_A - 1 - j:CONV_A - j, sl]
        y2 = _silu(acc)
        for i in range(2):
            which, h = divmod(2 * c + i, N_HEADS_A)
            y = y2[:, i * LANE:(i + 1) * LANE]
            if which < 2:
                y = y * lax.rsqrt(jnp.sum(y * y, axis=-1, keepdims=True) + 1e-6)
            if which == 0:
                y = y * (HEAD_DIM_A ** -0.5)
            outs[which][h] = y

    n_chunks = 3 * WIDTH_A // wide
    plain = [(gate_ref, wgate_ref, 0), (gate_ref, wgate_ref, 1), (u_ref, wu_ref, 0), (u_ref, wu_ref, 1)]
    pre_next = pre_qkv(0)
    ba = _dot(xn, wba_ref[...])
    for c in range(n_chunks):
        pre = pre_next
        if c + 1 < n_chunks:
            pre_next = pre_qkv(c + 1)
        if c < len(plain):
            dst, w_ref, i = plain[c]
            dst[:, i * wide:(i + 1) * wide] = _dot(xn, w_ref[:, i * wide:(i + 1) * wide])
        post_qkv(c, pre)

    sp_in = ba + dtb_ref[...]
    softplus = jnp.maximum(sp_in, 0.0) + jnp.log1p(jnp.exp(-jnp.abs(sp_in)))
    lane = lax.broadcasted_iota(jnp.int32, ba.shape, 1)
    bg_ref[...] = jnp.where(lane < N_HEADS_A, _sigmoid(ba), -jnp.exp(alog_ref[...]) * softplus)


def _mix_in(x, g, wqkv, wgate, wba, wu, conv, alog, dtb, tt):
    t = x.shape[1]
    rows = t * NB
    tm = tt * NB
    head = jax.ShapeDtypeStruct((N_HEADS_A, rows, HEAD_DIM_A), F32)
    head_spec = pl.BlockSpec((N_HEADS_A, tm, HEAD_DIM_A), lambda i: (0, i, 0))
    row_spec = lambda w: pl.BlockSpec((tm, w), lambda i: (i, 0))
    return pl.pallas_call(
        _mix_in_kernel,
        grid=(t // tt,),
        in_specs=[pl.BlockSpec((NB, tt, D_MODEL), lambda i: (0, i, 0)), _full((1, D_MODEL)),
                  _full(wqkv.shape), _full(wgate.shape), _full(wba.shape), _full(wu.shape),
                  _full(conv.shape), _full((1, LANE)), _full((1, LANE))],
        out_specs=[row_spec(D_MODEL), head_spec, head_spec, head_spec, row_spec(LANE), row_spec(WIDTH_A),
                   row_spec(SSM_WIDTH)],
        out_shape=[jax.ShapeDtypeStruct((rows, D_MODEL), F32), head, head, head,
                   jax.ShapeDtypeStruct((rows, LANE), F32),
                   jax.ShapeDtypeStruct((rows, WIDTH_A), F32),
                   jax.ShapeDtypeStruct((rows, SSM_WIDTH), F32)],
        scratch_shapes=[pltpu.VMEM(((CONV_A - 1) * NB, 3 * WIDTH_A), F32),
                        pltpu.VMEM((D_MODEL // LANE, tm, LANE), F32)],
        compiler_params=_params(1),
        name="mix_in",
    )(x, g, wqkv, wgate, wba, wu, conv, alog, dtb)


def _split3(x):
    hi = x.astype(BF16).astype(F32)
    mid = (x - hi).astype(BF16).astype(F32)
    lo = (x - hi - mid).astype(BF16).astype(F32)
    return hi, mid, lo


def _delta_kernel(q_ref, k_ref, v_ref, bg_ref, o_ref, s_ref):
    c2 = 2 * CHUNK
    n_pairs = NB // 2

    @pl.when(pl.program_id(1) == 0)
    def _():
        s_ref[...] = jnp.zeros_like(s_ref)

    row = lax.broadcasted_iota(jnp.int32, (c2, c2), 0)
    col = lax.broadcasted_iota(jnp.int32, (c2, c2), 1)
    same = (row >= CHUNK) == (col >= CHUNK)
    causal = same & (row >= col)
    strict = same & (row > col)
    tri = jnp.where(causal, 1.0, 0.0).astype(BF16)
    eye = jnp.where(row == col, 1.0, 0.0)
    lane = lax.broadcasted_iota(jnp.int32, (c2, LANE), 1)
    top = lax.broadcasted_iota(jnp.int32, (c2, LANE), 0) < CHUNK
    lane2 = lax.broadcasted_iota(jnp.int32, (1, 2 * HEAD_DIM_A), 1)

    def stacked(ref, p):
        return jnp.concatenate([ref[pl.ds(2 * p, CHUNK, stride=NB), :],
                                ref[pl.ds(2 * p + 1, CHUNK, stride=NB), :]], axis=0)

    chains = [(hh, p) for hh in range(DELTA_HEADS) for p in range(n_pairs)]
    n = len(chains)
    bg2 = [stacked(bg_ref, p) for p in range(n_pairs)]
    q2 = [stacked(q_ref.at[hh], p) for hh, p in chains]
    k2 = [stacked(k_ref.at[hh], p) for hh, p in chains]
    v2 = [stacked(v_ref.at[hh], p) for hh, p in chains]
    kb = [x.astype(BF16) for x in k2]
    kk = [_dot_nt(kb[i], kb[i]) for i in range(n)]
    qk = [_dot_nt(q2[i], kb[i]) for i in range(n)]

    beta, gc = [], []
    for hh, p in chains:
        h = pl.program_id(0) * DELTA_HEADS + hh
        beta.append(jnp.sum(jnp.where(lane == h, bg2[p], 0.0), axis=-1, keepdims=True))
        g = jnp.sum(jnp.where(lane == h + N_HEADS_A, bg2[p], 0.0), axis=-1, keepdims=True)
        ghi, gmid, glo = _split3(g)
        g3 = jnp.where(lane == 0, ghi, jnp.where(lane == 1, gmid, jnp.where(lane == 2, glo, 0.0)))
        gc.append(jnp.sum(jnp.dot(tri, g3.astype(BF16), preferred_element_type=F32), axis=-1, keepdims=True))

    pw, tinv, eg = [], [], []
    for i in range(n):
        gcb = jnp.broadcast_to(gc[i], (c2, c2))
        decay = jnp.where(causal, jnp.exp(jnp.where(causal, gcb - gcb.T, 0.0)), 0.0)
        eg.append(jnp.exp(gc[i]))
        pw.append(jnp.where(strict, kk[i] * decay * (-beta[i]), 0.0))
        qk[i] = jnp.where(causal, qk[i] * decay, 0.0)
        tinv.append(eye + pw[i])
    for _ in range(int(math.log2(CHUNK)) - 1):
        pw = [_dot(pw[i], pw[i]) for i in range(n)]
        tinv = [tinv[i] + _dot(tinv[i], pw[i]) for i in range(n)]
    sol = [_dot(tinv[i], jnp.concatenate([v2[i] * beta[i], k2[i] * (beta[i] * eg[i])], axis=1))
           for i in range(n)]

    s01 = [s_ref[i] for i in range(n)]
    ws = [_dot(sol[i][:, HEAD_DIM_A:], s01[i]) for i in range(n)]
    qs = [_dot(q2[i] * eg[i], s01[i]) for i in range(n)]
    vn = [sol[i][:, :HEAD_DIM_A] - jnp.where(top, ws[i][:, :HEAD_DIM_A], ws[i][:, HEAD_DIM_A:]) for i in range(n)]
    o2 = [jnp.where(top, qs[i][:, :HEAD_DIM_A], qs[i][:, HEAD_DIM_A:]) + _dot(qk[i], vn[i]) for i in range(n)]
    for i, (hh, p) in enumerate(chains):
        o_ref[hh, pl.ds(2 * p, CHUNK, stride=NB), :] = o2[i][:CHUNK]
        o_ref[hh, pl.ds(2 * p + 1, CHUNK, stride=NB), :] = o2[i][CHUNK:]
    for i in range(n):
        gl0, gl1 = gc[i][CHUNK - 1:CHUNK, :], gc[i][c2 - 1:c2, :]
        kd = k2[i] * jnp.exp(jnp.where(top[:, :1], gl0, gl1) - gc[i])
        vn01 = jnp.concatenate([jnp.where(top, vn[i], 0.0), jnp.where(top, 0.0, vn[i])], axis=1)
        sdec = jnp.where(lane2 < HEAD_DIM_A, jnp.exp(gl0), jnp.exp(gl1))
        s_ref[i] = s01[i] * sdec + _dot(kd.T, vn01)


def _delta(q, k, v, bg):
    rows = q.shape[1]
    cm = CHUNK * NB
    head_spec = pl.BlockSpec((DELTA_HEADS, cm, HEAD_DIM_A), lambda h, c: (h, c, 0))
    return pl.pallas_call(
        _delta_kernel,
        grid=(N_HEADS_A // DELTA_HEADS, rows // cm),
        in_specs=[head_spec, head_spec, head_spec, pl.BlockSpec((cm, LANE), lambda h, c: (c, 0))],
        out_specs=head_spec,
        out_shape=jax.ShapeDtypeStruct(q.shape, F32),
        scratch_shapes=[pltpu.VMEM((DELTA_HEADS * NB // 2, HEAD_DIM_A, 2 * HEAD_DIM_A), F32)],
        compiler_params=_params(2),
        name="delta_rule",
    )(q, k, v, bg)


def _gelu_tanh(x):
    return 0.5 * x * (1.0 + jnp.tanh(math.sqrt(2.0 / math.pi) * (x + 0.044715 * (x * x * x))))


def _s5_out_kernel(x_ref, u_ref, o_ref, gate_ref, bmat_ref, lre_ref, lim_ref, cmat_ref, d_ref,
                   wglu_ref, bglu_ref, onorm_ref, wout_ref, out_ref, st_ref, xs_ref):
    tm = x_ref.shape[0]
    tt = tm // NB
    ns = SUPER_STATE

    @pl.when(pl.program_id(0) == 0)
    def _():
        st_ref[...] = jnp.zeros_like(st_ref)

    u = u_ref[...]
    for j in range(N_SUPER):
        xs_ref[j] = _dot(u[:, j * LANE:(j + 1) * LANE], bmat_ref[j])

    gate = gate_ref[...]
    parts = []
    for h in range(N_HEADS_A):
        parts.append(_rms(o_ref[h], onorm_ref[...]) * _silu(gate[:, h * HEAD_DIM_A:(h + 1) * HEAD_DIM_A]))
    out_a = x_ref[...] + _dot(jnp.concatenate(parts, axis=1), wout_ref[:WIDTH_A, :])

    ys = []
    for j in range(N_SUPER):
        a_re = jnp.broadcast_to(lre_ref[:, j * ns:(j + 1) * ns], (NB, ns))
        a_im = jnp.broadcast_to(lim_ref[:, j * ns:(j + 1) * ns], (NB, ns))
        xr, xi = st_ref[j, :, :ns], st_ref[j, :, ns:]
        for t in range(tt):
            rows = slice(t * NB, (t + 1) * NB)
            xr, xi = (a_re * xr - a_im * xi + xs_ref[j, rows, :ns],
                      a_re * xi + a_im * xr + xs_ref[j, rows, ns:])
            xs_ref[j, rows, :ns] = xr
            xs_ref[j, rows, ns:] = xi
        st_ref[j, :, :ns] = xr
        st_ref[j, :, ns:] = xi
        ys.append(_dot(xs_ref[j], cmat_ref[j]))
    y = jnp.concatenate(ys, axis=1) + d_ref[...] * u
    y = _gelu_tanh(y)
    yb = y * _sigmoid(_dot(y, wglu_ref[...]) + bglu_ref[...])
    out_ref[...] = out_a + _dot(yb, wout_ref[WIDTH_A:, :])


def _s5_out(x2, u, o, gate, bmat, lre, lim, cmat, dvec, wglu, bglu, onorm, wout, tt):
    rows = x2.shape[0]
    tm = tt * NB
    row_spec = lambda w: pl.BlockSpec((tm, w), lambda i: (i, 0))
    return pl.pallas_call(
        _s5_out_kernel,
        grid=(rows // tm,),
        in_specs=[row_spec(D_MODEL), row_spec(SSM_WIDTH),
                  pl.BlockSpec((N_HEADS_A, tm, HEAD_DIM_A), lambda i: (0, i, 0)), row_spec(WIDTH_A),
                  _const(bmat.shape), _const(lre.shape), _const(lim.shape), _const(cmat.shape),
                  _const(dvec.shape), _const(wglu.shape), _const(bglu.shape), _const(onorm.shape),
                  _const(wout.shape)],
        out_specs=row_spec(D_MODEL),
        out_shape=jax.ShapeDtypeStruct(x2.shape, F32),
        scratch_shapes=[pltpu.VMEM((N_SUPER, NB, 2 * SUPER_STATE), F32),
                        pltpu.VMEM((N_SUPER, tm, 2 * SUPER_STATE), F32)],
        compiler_params=_params(1),
        name="s5_mix_out",
    )(x2, u, o, gate, bmat, lre, lim, cmat, dvec, wglu, bglu, onorm, wout)


def _pool_stage(x, tile, g_ref, pw_ref, ps_ref, carry_ref):
    tm = x.shape[0]
    tt = tm // NB
    hist = carry_ref.shape[0]

    @pl.when(tile == 0)
    def _():
        carry_ref[...] = jnp.zeros_like(carry_ref)

    xn = _rms(x, g_ref[...])
    tpos = tile * tt + lax.broadcasted_iota(jnp.int32, (tm, 1), 0) // NB + 1
    outs = []
    for gi, win in enumerate(POOL_WINDOWS):
        sl = slice(gi * POOL_GROUP, (gi + 1) * POOL_GROUP)
        xg = xn[:, sl]
        s = jnp.concatenate([carry_ref[:, sl], xg], axis=0)
        span = 1
        while span < win:
            s = s[NB * span:, :] + s[:s.shape[0] - NB * span, :]
            span *= 2
        s = s[s.shape[0] - tm:, :]
        count = jnp.minimum(tpos, win).astype(F32)
        mean = s * (1.0 / count)
        outs.append(x[:, sl] + _dot(mean - xg, pw_ref[gi]) * ps_ref[:, sl])
    carry_ref[...] = xn[tm - hist:, :]
    return jnp.concatenate(outs, axis=1)


def _xattn_kernel(*refs, with_pool):
    if with_pool:
        (x_ref, pg_ref, pw_ref, ps_ref, g_ref, wq_ref, wo_ref, k_ref, v_ref, out_ref,
         qs_ref, os_ref, carry_ref) = refs
    else:
        x_ref, g_ref, wq_ref, wo_ref, k_ref, v_ref, out_ref, qs_ref, os_ref = refs
    tm = x_ref.shape[0]
    tt = tm // NB
    x = x_ref[...]
    if with_pool:
        x = _pool_stage(x, pl.program_id(0), pg_ref, pw_ref, ps_ref, carry_ref)
    q = _dot(_rms(x, g_ref[...]), wq_ref[...]) * (HEAD_DIM_X ** -0.5)
    n_slab = D_MODEL // LANE
    for s in range(n_slab):
        qs_ref[s] = q[:, s * LANE:(s + 1) * LANE]
    per_head = HEAD_DIM_X // LANE
    heads = [slice(h * HEAD_DIM_X, (h + 1) * HEAD_DIM_X) for h in range(N_HEADS_X)]

    def scores(b):
        rows = pl.ds(b, tt, stride=NB)
        out = []
        for h in range(N_HEADS_X):
            qbh = jnp.concatenate([qs_ref[per_head * h + i, rows, :] for i in range(per_head)], axis=1)
            out.append(_dot_nt(qbh, k_ref[b, :, heads[h]]))
        return out

    sc_next = scores(0)
    for b in range(NB):
        sc = sc_next
        if b + 1 < NB:
            sc_next = scores(b + 1)
        e = [jnp.exp(s - jnp.max(s, axis=-1, keepdims=True)) for s in sc]
        rows = pl.ds(b, tt, stride=NB)
        for h in range(N_HEADS_X):
            obh = _dot(e[h], v_ref[b, :, heads[h]]) / jnp.sum(e[h], axis=-1, keepdims=True)
            for i in range(per_head):
                os_ref[per_head * h + i, rows, :] = obh[:, i * LANE:(i + 1) * LANE]
    o = jnp.concatenate([os_ref[s] for s in range(n_slab)], axis=1)
    out_ref[...] = x + _dot(o, wo_ref[...])


def _xattn(x2, g, wq, wo, k, v, tt, pool=None):
    rows = x2.shape[0]
    tm = tt * NB
    row_spec = pl.BlockSpec((tm, D_MODEL), lambda i: (i, 0))
    slabs = pltpu.VMEM((D_MODEL // LANE, tm, LANE), F32)
    args = [g, wq, wo, k, v]
    scratch = [slabs, slabs]
    if pool is not None:
        args = list(pool) + args
        scratch.append(pltpu.VMEM((POOL_WINDOWS[-1] * NB, D_MODEL), F32))
    return pl.pallas_call(
        functools.partial(_xattn_kernel, with_pool=pool is not None),
        grid=(rows // tm,),
        in_specs=[row_spec] + [_const(a.shape) for a in args],
        out_specs=row_spec,
        out_shape=jax.ShapeDtypeStruct(x2.shape, F32),
        scratch_shapes=scratch,
        compiler_params=_params(1),
        name="mem_xattn",
    )(x2, *args)


def _ffn_kernel(x_ref, g_ref, wg_ref, wv_ref, cg_ref, cv_ref, wd_ref, gf_ref, out_ref,
                carry_g_ref, carry_v_ref, acc_ref, *slab_refs, final_norm):
    tm = x_ref.shape[0]
    hist = (CONV_FFN - 1) * NB

    @pl.when(pl.program_id(0) == 0)
    def _():
        carry_g_ref[...] = jnp.zeros_like(carry_g_ref)
        carry_v_ref[...] = jnp.zeros_like(carry_v_ref)

    x = x_ref[...]
    xn = _rms(x, g_ref[...]).astype(BF16)

    def conv(pre, c_ref, carry_ref, sl):
        ext = jnp.concatenate([carry_ref[:, sl], pre], axis=0)
        carry_ref[:, sl] = pre[tm - hist:, :]
        acc = ext[hist:, :] * c_ref[CONV_FFN - 1:CONV_FFN, sl]
        for j in range(1, CONV_FFN):
            acc = acc + ext[hist - NB * j: hist - NB * j + tm, :] * c_ref[CONV_FFN - 1 - j:CONV_FFN - j, sl]
        return acc

    def up(c):
        sl = slice(c * FF_CHUNK, (c + 1) * FF_CHUNK)
        return _dot(xn, wg_ref[:, sl]), _dot(xn, wv_ref[:, sl])

    n_chunks = D_FF // FF_CHUNK
    pre_next = up(0)
    for c in range(n_chunks):
        sl = slice(c * FF_CHUNK, (c + 1) * FF_CHUNK)
        pre_g, pre_v = pre_next
        if c + 1 < n_chunks:
            pre_next = up(c + 1)
        act = _silu(conv(pre_g, cg_ref, carry_g_ref, sl)) * conv(pre_v, cv_ref, carry_v_ref, sl)
        part = _dot(act, wd_ref[sl, :])
        if c == 0:
            acc_ref[...] = part
        else:
            acc_ref[...] += part
    y = x + acc_ref[...]
    if not final_norm:
        out_ref[...] = y
    else:
        y = _rms(y, gf_ref[...])
        tt = tm // NB
        ys_ref, = slab_refs
        for s in range(D_MODEL // LANE):
            ys_ref[s] = y[:, s * LANE:(s + 1) * LANE]
        for b in range(NB):
            for s in range(D_MODEL // LANE):
                out_ref[b, :, s * LANE:(s + 1) * LANE] = ys_ref[s, pl.ds(b, tt, stride=NB), :]


def _ffn(x2, g, wg, wv, cg, cv, wd, gf, tt, final_norm):
    rows = x2.shape[0]
    tm = tt * NB
    row_spec = pl.BlockSpec((tm, D_MODEL), lambda i: (i, 0))
    hist = (CONV_FFN - 1) * NB
    scratch = [pltpu.VMEM((hist, D_FF), F32), pltpu.VMEM((hist, D_FF), F32), pltpu.VMEM((tm, D_MODEL), F32)]
    if final_norm:
        out_spec = pl.BlockSpec((NB, tt, D_MODEL), lambda i: (0, i, 0))
        out_shape = jax.ShapeDtypeStruct((NB, rows // NB, D_MODEL), F32)
        scratch.append(pltpu.VMEM((D_MODEL // LANE, tm, LANE), F32))
    else:
        out_spec, out_shape = row_spec, jax.ShapeDtypeStruct(x2.shape, F32)
    return pl.pallas_call(
        functools.partial(_ffn_kernel, final_norm=final_norm),
        grid=(rows // tm,),
        in_specs=[row_spec, _const((1, D_MODEL)), _const(wg.shape), _const(wv.shape), _const(cg.shape),
                  _const(cv.shape), _const(wd.shape), _const((1, D_MODEL))],
        out_specs=out_spec,
        out_shape=out_shape,
        scratch_shapes=scratch,
        compiler_params=_params(1),
        name="conv_ffn",
    )(x2, g, wg, wv, cg, cv, wd, gf)


def _s5_tables(lam_re, lam_im, b_re, b_im, c_re, c_im, log_dt):
    dt = jnp.exp(log_dt)[:, None]
    mag = jnp.exp(lam_re * dt)
    ang = lam_im * dt
    lb_re, lb_im = mag * jnp.cos(ang), mag * jnp.sin(ang)
    den = lam_re * lam_re + lam_im * lam_im
    nr, ni = lb_re - 1.0, lb_im
    coef_re = (nr * lam_re + ni * lam_im) / den
    coef_im = (ni * lam_re - nr * lam_im) / den
    bb_re = coef_re[..., None] * b_re - coef_im[..., None] * b_im
    bb_im = coef_re[..., None] * b_im + coef_im[..., None] * b_re
    eye = jnp.eye(8, dtype=F32)

    def in_blocks(bb):
        bb = bb.reshape(N_SUPER, 8, SSM_STATE, SSM_GROUP)
        return jnp.einsum('jgph,gm->jghmp', bb, eye).reshape(N_SUPER, LANE, SUPER_STATE)

    def out_blocks(cc):
        cc = cc.reshape(N_SUPER, 8, SSM_GROUP, SSM_STATE)
        return jnp.einsum('jghp,gm->jgpmh', cc, eye).reshape(N_SUPER, SUPER_STATE, LANE)

    bmat = jnp.concatenate([in_blocks(bb_re), in_blocks(bb_im)], axis=2).astype(BF16)
    cmat = jnp.concatenate([out_blocks(c_re), -out_blocks(c_im)], axis=1).astype(BF16)
    return bmat, cmat, lb_re.reshape(1, -1), lb_im.reshape(1, -1)


def _pad_lanes(v, offset):
    return jnp.zeros((1, LANE), F32).at[0, offset:offset + v.shape[0]].set(v)


def kernel(x, mem, norm_mix_g, norm_xa_g, norm_ffn_g, norm_mem_g, norm_final_g, w_in_ab, conv_qkv_a, a_log_a, dt_bias_a, onorm_g_a, ssm_lambda_re, ssm_lambda_im, ssm_b_re, ssm_b_im, ssm_c_re, ssm_c_im, ssm_d, ssm_log_dt, w_glu_b, b_glu_b, w_out_ab, pool_w, pool_scale, xa_wq, xa_wkv, xa_wo, ffn_w_up, ffn_conv, ffn_w_down):
    b, t, d = x.shape
    assert b == NB and d == D_MODEL and t % CHUNK == 0
    tt = min(CHUNK, t)
    tt_xa = min(XATTN_TT, t)
    row = lambda v: v.reshape(1, -1).astype(F32)

    k_mem, v_mem = _memkv(mem, row(norm_mem_g), xa_wkv.astype(BF16))

    pool = None
    for layer in range(2):
        if layer == 0:
            w_in = w_in_ab[0]
            qkv_w = 3 * WIDTH_A
            wba = jnp.zeros((d, LANE), F32).at[:, :2 * N_HEADS_A].set(
                w_in[:, qkv_w + WIDTH_A: qkv_w + WIDTH_A + 2 * N_HEADS_A])
            x2, q, k, v, bg, gate, u = _mix_in(
                x, row(norm_mix_g[0]), w_in[:, :qkv_w].astype(BF16),
                w_in[:, qkv_w:qkv_w + WIDTH_A].astype(BF16), wba.astype(BF16),
                w_in[:, qkv_w + WIDTH_A + 2 * N_HEADS_A:].astype(BF16), conv_qkv_a[0],
                _pad_lanes(a_log_a[0], N_HEADS_A), _pad_lanes(dt_bias_a[0], N_HEADS_A), tt)
            o = _delta(q, k, v, bg)
            bmat, cmat, lre, lim = _s5_tables(ssm_lambda_re[0], ssm_lambda_im[0], ssm_b_re[0], ssm_b_im[0],
                                              ssm_c_re[0], ssm_c_im[0], ssm_log_dt[0])
            x2 = _s5_out(x2, u, o, gate, bmat, lre, lim, cmat, row(ssm_d[0]), w_glu_b[0].astype(BF16),
                         row(b_glu_b[0]), row(onorm_g_a[0]), w_out_ab[0].astype(BF16), tt)
        else:
            pool = (row(norm_mix_g[1]), pool_w[0].astype(BF16), row(pool_scale[0]))
        x2 = _xattn(x2, row(norm_xa_g[layer]), xa_wq[layer].astype(BF16), xa_wo[layer].astype(BF16),
                    k_mem[layer], v_mem[layer], tt_xa, pool=pool)
        x2 = _ffn(x2, row(norm_ffn_g[layer]), ffn_w_up[layer][:, :D_FF].astype(BF16),
                  ffn_w_up[layer][:, D_FF:].astype(BF16), ffn_conv[layer][:, :D_FF], ffn_conv[layer][:, D_FF:],
                  ffn_w_down[layer].astype(BF16), row(norm_final_g), tt, final_norm=(layer == 1))
    return x2
```

```python
import functools
import math

import jax
import jax.numpy as jnp
from jax import lax
from jax.experimental import pallas as pl
from jax.experimental.pallas import tpu as pltpu

F32 = jnp.float32
BF16 = jnp.bfloat16

NB = 8
D_MODEL = 1024
LANE = 128
N_HEADS_A = 4
HEAD_DIM_A = 128
WIDTH_A = 512
CONV_A = 4
CHUNK = 64
DELTA_HEADS = 4
SSM_WIDTH = 512
SSM_GROUP = 16
N_GROUPS = 32
SSM_STATE = 64
N_SUPER = 4
SUPER_STATE = 8 * SSM_STATE
POOL_WINDOWS = (2, 4, 8, 16)
POOL_GROUP = 256
MEM_LEN = 256
N_HEADS_X = 4
HEAD_DIM_X = 256
XATTN_TT = 128
XATTN_LOOKAHEAD = 2
D_FF = 2816
CONV_FFN = 3
FFN_TT = 64
FF_CHUNKS = (512, 512, 512, 512, 512, 256)
RMS_EPS = 1e-6
VMEM_LIMIT = 56 * 1024 * 1024


def _dot(a, b):
    return jnp.dot(a.astype(BF16), b.astype(BF16), preferred_element_type=F32)


def _dot_nt(a, b):
    return lax.dot_general(a.astype(BF16), b.astype(BF16), (((1,), (1,)), ((), ())),
                           preferred_element_type=F32)


def _rms(x, g):
    return x * lax.rsqrt(jnp.mean(x * x, axis=-1, keepdims=True) + RMS_EPS) * g


def _sigmoid(x):
    return 1.0 / (1.0 + jnp.exp(-x))


def _silu(x):
    return x * _sigmoid(x)


def _params(n_axes=1, flags=None):
    return pltpu.CompilerParams(dimension_semantics=("arbitrary",) * n_axes,
                                vmem_limit_bytes=VMEM_LIMIT, flags=flags)


def _full(shape):
    return pl.BlockSpec(shape, lambda *_: (0,) * len(shape))


def _const(shape):
    return pl.BlockSpec(shape, lambda *_: (0,) * len(shape), pipeline_mode=pl.Buffered(1))


def _memkv_kernel(mem_ref, g_ref, wkv_ref, k_ref, v_ref):
    mn = _rms(mem_ref[...], g_ref[...])
    kv = _dot(mn, wkv_ref[...])
    k_ref[...] = kv[:, :D_MODEL].T.astype(BF16)
    v_ref[...] = kv[:, D_MODEL:].astype(BF16)


def _memkv(mem, g, wkv):
    n_layers = wkv.shape[0]
    out = jax.ShapeDtypeStruct((n_layers, NB, MEM_LEN, D_MODEL), BF16)
    return pl.pallas_call(
        _memkv_kernel,
        grid=(n_layers, NB),
        in_specs=[pl.BlockSpec((None, MEM_LEN, D_MODEL), lambda l, b: (b, 0, 0)),
                  pl.BlockSpec((1, D_MODEL), lambda l, b: (0, 0)),
                  pl.BlockSpec((None, D_MODEL, 2 * D_MODEL), lambda l, b: (l, 0, 0))],
        out_specs=[pl.BlockSpec((None, None, D_MODEL, MEM_LEN), lambda l, b: (l, b, 0, 0)),
                   pl.BlockSpec((None, None, MEM_LEN, D_MODEL), lambda l, b: (l, b, 0, 0))],
        out_shape=[jax.ShapeDtypeStruct((n_layers, NB, D_MODEL, MEM_LEN), BF16), out],
        compiler_params=_params(2),
        name="memkv",
    )(mem, g, wkv)


def _mix_in_kernel(x_ref, g_ref, wqkv_ref, wgate_ref, wba_ref, wu_ref, conv_ref, alog_ref, dtb_ref,
                   x2_ref, q_ref, k_ref, v_ref, bg_ref, gate_ref, u_ref, carry_ref, xs_ref):
    tt = x_ref.shape[1]
    tm = tt * NB
    hist = (CONV_A - 1) * NB
    wide = 2 * LANE

    @pl.when(pl.program_id(0) == 0)
    def _():
        carry_ref[...] = jnp.zeros_like(carry_ref)

    for b in range(NB):
        for s in range(D_MODEL // LANE):
            xs_ref[s, pl.ds(b, tt, stride=NB), :] = x_ref[b, :, s * LANE:(s + 1) * LANE]
    x = jnp.concatenate([xs_ref[s] for s in range(D_MODEL // LANE)], axis=1)
    x2_ref[...] = x
    xn = _rms(x, g_ref[...]).astype(BF16)

    def pre_qkv(c):
        return _dot(xn, wqkv_ref[:, c * wide:(c + 1) * wide])

    outs = (q_ref, k_ref, v_ref)

    def post_qkv(c, pre):
        sl = slice(c * wide, (c + 1) * wide)
        ext = jnp.concatenate([carry_ref[:, sl], pre], axis=0)
        carry_ref[:, sl] = pre[tm - hist:, :]
        acc = ext[hist:, :] * conv_ref[CONV_A - 1:CONV_A, sl]
        for j in range(1, CONV_A):
            acc = acc + ext[hist - NB * j: hist - NB * j + tm, :] * conv_ref[CONV_A - 1 - j:CONV_A - j, sl]
        y2 = _silu(acc)
        for i in range(2):
            which, h = divmod(2 * c + i, N_HEADS_A)
            y = y2[:, i * LANE:(i + 1) * LANE]
            if which < 2:
                y = y * lax.rsqrt(jnp.sum(y * y, axis=-1, keepdims=True) + 1e-6)
            if which == 0:
                y = y * (HEAD_DIM_A ** -0.5)
            outs[which][h] = y

    n_chunks = 3 * WIDTH_A // wide
    plain = [(gate_ref, wgate_ref, 0), (gate_ref, wgate_ref, 1), (u_ref, wu_ref, 0), (u_ref, wu_ref, 1)]
    pre_next = pre_qkv(0)
    ba = _dot(xn, wba_ref[...])
    for c in range(n_chunks):
        pre = pre_next
        if c + 1 < n_chunks:
            pre_next = pre_qkv(c + 1)
        if c < len(plain):
            dst, w_ref, i = plain[c]
            dst[:, i * wide:(i + 1) * wide] = _dot(xn, w_ref[:, i * wide:(i + 1) * wide])
        post_qkv(c, pre)

    sp_in = ba + dtb_ref[...]
    softplus = jnp.maximum(sp_in, 0.0) + jnp.log1p(jnp.exp(-jnp.abs(sp_in)))
    lane = lax.broadcasted_iota(jnp.int32, ba.shape, 1)
    bg_ref[...] = jnp.where(lane < N_HEADS_A, _sigmoid(ba), -jnp.exp(alog_ref[...]) * softplus)


def _mix_in(x, g, wqkv, wgate, wba, wu, conv, alog, dtb, tt):
    t = x.shape[1]
    rows = t * NB
    tm = tt * NB
    head = jax.ShapeDtypeStruct((N_HEADS_A, rows, HEAD_DIM_A), F32)
    head_spec = pl.BlockSpec((N_HEADS_A, tm, HEAD_DIM_A), lambda i: (0, i, 0))
    row_spec = lambda w: pl.BlockSpec((tm, w), lambda i: (i, 0))
    return pl.pallas_call(
        _mix_in_kernel,
        grid=(t // tt,),
        in_specs=[pl.BlockSpec((NB, tt, D_MODEL), lambda i: (0, i, 0)), _full((1, D_MODEL)),
                  _full(wqkv.shape), _full(wgate.shape), _full(wba.shape), _full(wu.shape),
                  _full(conv.shape), _full((1, LANE)), _full((1, LANE))],
        out_specs=[row_spec(D_MODEL), head_spec, head_spec, head_spec, row_spec(LANE), row_spec(WIDTH_A),
                   row_spec(SSM_WIDTH)],
        out_shape=[jax.ShapeDtypeStruct((rows, D_MODEL), F32), head, head, head,
                   jax.ShapeDtypeStruct((rows, LANE), F32),
                   jax.ShapeDtypeStruct((rows, WIDTH_A), F32),
                   jax.ShapeDtypeStruct((rows, SSM_WIDTH), F32)],
        scratch_shapes=[pltpu.VMEM(((CONV_A - 1) * NB, 3 * WIDTH_A), F32),
                        pltpu.VMEM((D_MODEL // LANE, tm, LANE), F32)],
        compiler_params=_params(1),
        name="mix_in",
    )(x, g, wqkv, wgate, wba, wu, conv, alog, dtb)


def _split3(x):
    hi = x.astype(BF16).astype(F32)
    mid = (x - hi).astype(BF16).astype(F32)
    lo = (x - hi - mid).astype(BF16).astype(F32)
    return hi, mid, lo


def _delta_kernel(q_ref, k_ref, v_ref, bg_ref, o_ref, s_ref):
    c2 = 2 * CHUNK
    n_pairs = NB // 2

    @pl.when(pl.program_id(1) == 0)
    def _():
        s_ref[...] = jnp.zeros_like(s_ref)

    row = lax.broadcasted_iota(jnp.int32, (c2, c2), 0)
    col = lax.broadcasted_iota(jnp.int32, (c2, c2), 1)
    same = (row >= CHUNK) == (col >= CHUNK)
    causal = same & (row >= col)
    strict = same & (row > col)
    tri = jnp.where(causal, 1.0, 0.0).astype(BF16)
    eye = jnp.where(row == col, 1.0, 0.0)
    lane = lax.broadcasted_iota(jnp.int32, (c2, LANE), 1)
    top = lax.broadcasted_iota(jnp.int32, (c2, LANE), 0) < CHUNK
    lane2 = lax.broadcasted_iota(jnp.int32, (1, 2 * HEAD_DIM_A), 1)

    def stacked(ref, p):
        return jnp.concatenate([ref[pl.ds(2 * p, CHUNK, stride=NB), :],
                                ref[pl.ds(2 * p + 1, CHUNK, stride=NB), :]], axis=0)

    chains = [(hh, p) for hh in range(DELTA_HEADS) for p in range(n_pairs)]
    n = len(chains)
    bg2 = [stacked(bg_ref, p) for p in range(n_pairs)]
    q2 = [stacked(q_ref.at[hh], p) for hh, p in chains]
    k2 = [stacked(k_ref.at[hh], p) for hh, p in chains]
    v2 = [stacked(v_ref.at[hh], p) for hh, p in chains]
    kt = [x.T.astype(BF16) for x in k2]
    kk = [_dot(k2[i], kt[i]) for i in range(n)]
    qk = [_dot(q2[i], kt[i]) for i in range(n)]

    beta, gc = [], []
    for hh, p in chains:
        h = pl.program_id(0) * DELTA_HEADS + hh
        beta.append(jnp.sum(jnp.where(lane == h, bg2[p], 0.0), axis=-1, keepdims=True))
        g = jnp.sum(jnp.where(lane == h + N_HEADS_A, bg2[p], 0.0), axis=-1, keepdims=True)
        ghi, gmid, glo = _split3(g)
        g3 = jnp.where(lane == 0, ghi, jnp.where(lane == 1, gmid, jnp.where(lane == 2, glo, 0.0)))
        gc.append(jnp.sum(jnp.dot(tri, g3.astype(BF16), preferred_element_type=F32), axis=-1, keepdims=True))

    pw, tinv, eg = [], [], []
    for i in range(n):
        gcb = jnp.broadcast_to(gc[i], (c2, c2))
        decay = jnp.where(causal, jnp.exp(jnp.where(causal, gcb - gcb.T, 0.0)), 0.0)
        eg.append(jnp.exp(gc[i]))
        pw.append(jnp.where(strict, kk[i] * decay * (-beta[i]), 0.0))
        qk[i] = jnp.where(causal, qk[i] * decay, 0.0)
        tinv.append(eye + pw[i])
    for _ in range(int(math.log2(CHUNK)) - 1):
        pw = [_dot(pw[i], pw[i]) for i in range(n)]
        tinv = [tinv[i] + _dot(tinv[i], pw[i]) for i in range(n)]
    sol = [_dot(tinv[i], jnp.concatenate([v2[i] * beta[i], k2[i] * (beta[i] * eg[i])], axis=1))
           for i in range(n)]

    s01 = [s_ref[i] for i in range(n)]
    ws = [_dot(sol[i][:, HEAD_DIM_A:], s01[i]) for i in range(n)]
    qs = [_dot(q2[i] * eg[i], s01[i]) for i in range(n)]
    vn = [sol[i][:, :HEAD_DIM_A] - jnp.where(top, ws[i][:, :HEAD_DIM_A], ws[i][:, HEAD_DIM_A:]) for i in range(n)]
    o2 = [jnp.where(top, qs[i][:, :HEAD_DIM_A], qs[i][:, HEAD_DIM_A:]) + _dot(qk[i], vn[i]) for i in range(n)]
    for i, (hh, p) in enumerate(chains):
        o_ref[hh, pl.ds(2 * p, CHUNK, stride=NB), :] = o2[i][:CHUNK]
        o_ref[hh, pl.ds(2 * p + 1, CHUNK, stride=NB), :] = o2[i][CHUNK:]
    for i in range(n):
        gl0, gl1 = gc[i][CHUNK - 1:CHUNK, :], gc[i][c2 - 1:c2, :]
        kd = k2[i] * jnp.exp(jnp.where(top[:, :1], gl0, gl1) - gc[i])
        vn01 = jnp.concatenate([jnp.where(top, vn[i], 0.0), jnp.where(top, 0.0, vn[i])], axis=1)
        sdec = jnp.where(lane2 < HEAD_DIM_A, jnp.exp(gl0), jnp.exp(gl1))
        s_ref[i] = s01[i] * sdec + _dot(kd.T, vn01)


def _delta(q, k, v, bg):
    rows = q.shape[1]
    cm = CHUNK * NB
    head_spec = pl.BlockSpec((DELTA_HEADS, cm, HEAD_DIM_A), lambda h, c: (h, c, 0))
    return pl.pallas_call(
        _delta_kernel,
        grid=(N_HEADS_A // DELTA_HEADS, rows // cm),
        in_specs=[head_spec, head_spec, head_spec, pl.BlockSpec((cm, LANE), lambda h, c: (c, 0))],
        out_specs=head_spec,
        out_shape=jax.ShapeDtypeStruct(q.shape, F32),
        scratch_shapes=[pltpu.VMEM((DELTA_HEADS * NB // 2, HEAD_DIM_A, 2 * HEAD_DIM_A), F32)],
        compiler_params=_params(2),
        name="delta_rule",
    )(q, k, v, bg)


def _gelu_tanh(x):
    return 0.5 * x * (1.0 + jnp.tanh(math.sqrt(2.0 / math.pi) * (x + 0.044715 * (x * x * x))))


def _s5_out_kernel(x_ref, u_ref, o_ref, gate_ref, bmat_ref, lre_ref, lim_ref, cmat_ref, d_ref,
                   wglu_ref, bglu_ref, onorm_ref, wout_ref, out_ref, st_ref, xs_ref):
    tm = x_ref.shape[0]
    tt = tm // NB
    ns = SUPER_STATE

    @pl.when(pl.program_id(0) == 0)
    def _():
        st_ref[...] = jnp.zeros_like(st_ref)

    u = u_ref[...]
    for j in range(N_SUPER):
        xs_ref[j] = _dot(u[:, j * LANE:(j + 1) * LANE], bmat_ref[j])

    gate = gate_ref[...]
    parts = []
    for h in range(N_HEADS_A):
        parts.append(_rms(o_ref[h], onorm_ref[...]) * _silu(gate[:, h * HEAD_DIM_A:(h + 1) * HEAD_DIM_A]))
    out_a = x_ref[...] + _dot(jnp.concatenate(parts, axis=1), wout_ref[:WIDTH_A, :])

    ys = []
    for j in range(N_SUPER):
        a_re = jnp.broadcast_to(lre_ref[:, j * ns:(j + 1) * ns], (NB, ns))
        a_im = jnp.broadcast_to(lim_ref[:, j * ns:(j + 1) * ns], (NB, ns))
        xr, xi = st_ref[j, :, :ns], st_ref[j, :, ns:]
        for t in range(tt):
            rows = slice(t * NB, (t + 1) * NB)
            xr, xi = (a_re * xr - a_im * xi + xs_ref[j, rows, :ns],
                      a_re * xi + a_im * xr + xs_ref[j, rows, ns:])
            xs_ref[j, rows, :ns] = xr
            xs_ref[j, rows, ns:] = xi
        st_ref[j, :, :ns] = xr
        st_ref[j, :, ns:] = xi
        ys.append(_dot(xs_ref[j], cmat_ref[j]))
    y = jnp.concatenate(ys, axis=1) + d_ref[...] * u
    y = _gelu_tanh(y)
    yb = y * _sigmoid(_dot(y, wglu_ref[...]) + bglu_ref[...])
    out_ref[...] = out_a + _dot(yb, wout_ref[WIDTH_A:, :])


def _s5_out(x2, u, o, gate, bmat, lre, lim, cmat, dvec, wglu, bglu, onorm, wout, tt):
    rows = x2.shape[0]
    tm = tt * NB
    row_spec = lambda w: pl.BlockSpec((tm, w), lambda i: (i, 0))
    return pl.pallas_call(
        _s5_out_kernel,
        grid=(rows // tm,),
        in_specs=[row_spec(D_MODEL), row_spec(SSM_WIDTH),
                  pl.BlockSpec((N_HEADS_A, tm, HEAD_DIM_A), lambda i: (0, i, 0)), row_spec(WIDTH_A),
                  _const(bmat.shape), _const(lre.shape), _const(lim.shape), _const(cmat.shape),
                  _const(dvec.shape), _const(wglu.shape), _const(bglu.shape), _const(onorm.shape),
                  _const(wout.shape)],
        out_specs=row_spec(D_MODEL),
        out_shape=jax.ShapeDtypeStruct(x2.shape, F32),
        scratch_shapes=[pltpu.VMEM((N_SUPER, NB, 2 * SUPER_STATE), F32),
                        pltpu.VMEM((N_SUPER, tm, 2 * SUPER_STATE), F32)],
        compiler_params=_params(1),
        name="s5_mix_out",
    )(x2, u, o, gate, bmat, lre, lim, cmat, dvec, wglu, bglu, onorm, wout)


def _pool_stage(x, tile, g_ref, pw_ref, ps_ref, carry_ref):
    tm = x.shape[0]
    tt = tm // NB
    hist = carry_ref.shape[0]

    @pl.when(tile == 0)
    def _():
        carry_ref[...] = jnp.zeros_like(carry_ref)

    xn = _rms(x, g_ref[...])
    tpos = tile * tt + lax.broadcasted_iota(jnp.int32, (tm, 1), 0) // NB + 1
    outs = []
    for gi, win in enumerate(POOL_WINDOWS):
        sl = slice(gi * POOL_GROUP, (gi + 1) * POOL_GROUP)
        xg = xn[:, sl]
        s = jnp.concatenate([carry_ref[:, sl], xg], axis=0)
        span = 1
        while span < win:
            s = s[NB * span:, :] + s[:s.shape[0] - NB * span, :]
            span *= 2
        s = s[s.shape[0] - tm:, :]
        count = jnp.minimum(tpos, win).astype(F32)
        mean = s * (1.0 / count)
        outs.append(x[:, sl] + _dot(mean - xg, pw_ref[gi]) * ps_ref[:, sl])
    carry_ref[...] = xn[tm - hist:, :]
    return jnp.concatenate(outs, axis=1)


def _xattn_kernel(*refs, with_pool):
    if with_pool:
        (x_ref, pg_ref, pw_ref, ps_ref, g_ref, wq_ref, wo_ref, k_ref, v_ref, out_ref,
         qs_ref, os_ref, carry_ref) = refs
    else:
        x_ref, g_ref, wq_ref, wo_ref, k_ref, v_ref, out_ref, qs_ref, os_ref = refs
    tm = x_ref.shape[0]
    tt = tm // NB
    x = x_ref[...]
    if with_pool:
        x = _pool_stage(x, pl.program_id(0), pg_ref, pw_ref, ps_ref, carry_ref)
    q = _dot(_rms(x, g_ref[...]), wq_ref[...]) * (HEAD_DIM_X ** -0.5)
    n_slab = D_MODEL // LANE
    for s in range(n_slab):
        qs_ref[s] = q[:, s * LANE:(s + 1) * LANE]
    per_head = HEAD_DIM_X // LANE
    heads = [slice(h * HEAD_DIM_X, (h + 1) * HEAD_DIM_X) for h in range(N_HEADS_X)]

    def scores(b):
        rows = pl.ds(b, tt, stride=NB)
        out = []
        for h in range(N_HEADS_X):
            qbh = jnp.concatenate([qs_ref[per_head * h + i, rows, :] for i in range(per_head)], axis=1)
            out.append(_dot(qbh, k_ref[b, heads[h], :]))
        return out

    ahead = [scores(b) for b in range(XATTN_LOOKAHEAD)]
    for b in range(NB):
        sc = ahead.pop(0)
        if b + XATTN_LOOKAHEAD < NB:
            ahead.append(scores(b + XATTN_LOOKAHEAD))
        e = [jnp.exp(s - jnp.max(s, axis=-1, keepdims=True)) for s in sc]
        rows = pl.ds(b, tt, stride=NB)
        for h in range(N_HEADS_X):
            obh = _dot(e[h], v_ref[b, :, heads[h]]) / jnp.sum(e[h], axis=-1, keepdims=True)
            for i in range(per_head):
                os_ref[per_head * h + i, rows, :] = obh[:, i * LANE:(i + 1) * LANE]
    o = jnp.concatenate([os_ref[s] for s in range(n_slab)], axis=1)
    out_ref[...] = x + _dot(o, wo_ref[...])


def _xattn(x2, g, wq, wo, k, v, tt, pool=None):
    rows = x2.shape[0]
    tm = tt * NB
    row_spec = pl.BlockSpec((tm, D_MODEL), lambda i: (i, 0))
    slabs = pltpu.VMEM((D_MODEL // LANE, tm, LANE), F32)
    args = [g, wq, wo, k, v]
    scratch = [slabs, slabs]
    if pool is not None:
        args = list(pool) + args
        scratch.append(pltpu.VMEM((POOL_WINDOWS[-1] * NB, D_MODEL), F32))
    return pl.pallas_call(
        functools.partial(_xattn_kernel, with_pool=pool is not None),
        grid=(rows // tm,),
        in_specs=[row_spec] + [_const(a.shape) for a in args],
        out_specs=row_spec,
        out_shape=jax.ShapeDtypeStruct(x2.shape, F32),
        scratch_shapes=scratch,
        compiler_params=_params(1),
        name="mem_xattn",
    )(x2, *args)


def _ffn_kernel(x_ref, g_ref, wg_ref, wv_ref, cg_ref, cv_ref, wd_ref, gf_ref, out_ref,
                carry_g_ref, carry_v_ref, acc_ref, xn_ref, *slab_refs, final_norm):
    tm = x_ref.shape[0]
    hist = (CONV_FFN - 1) * NB

    @pl.when(pl.program_id(0) == 0)
    def _():
        carry_g_ref[...] = jnp.zeros_like(carry_g_ref)
        carry_v_ref[...] = jnp.zeros_like(carry_v_ref)

    x = x_ref[...]
    xn_ref[...] = _rms(x, g_ref[...]).astype(BF16)

    def conv(pre, c_ref, carry_ref, sl):
        ext = jnp.concatenate([carry_ref[:, sl], pre], axis=0)
        carry_ref[:, sl] = pre[tm - hist:, :]
        acc = ext[hist:, :] * c_ref[CONV_FFN - 1:CONV_FFN, sl]
        for j in range(1, CONV_FFN):
            acc = acc + ext[hist - NB * j: hist - NB * j + tm, :] * c_ref[CONV_FFN - 1 - j:CONV_FFN - j, sl]
        return acc

    bounds = [sum(FF_CHUNKS[:c]) for c in range(len(FF_CHUNKS) + 1)]
    assert bounds[-1] == D_FF

    def up(c):
        sl = slice(bounds[c], bounds[c + 1])
        return _dot(xn_ref[...], wg_ref[:, sl]), _dot(xn_ref[...], wv_ref[:, sl])

    n_chunks = len(FF_CHUNKS)
    pre_next = up(0)
    for c in range(n_chunks):
        sl = slice(bounds[c], bounds[c + 1])
        pre_g, pre_v = pre_next
        if c + 1 < n_chunks:
            pre_next = up(c + 1)
        act = _silu(conv(pre_g, cg_ref, carry_g_ref, sl)) * conv(pre_v, cv_ref, carry_v_ref, sl)
        part = _dot(act, wd_ref[sl, :])
        if c == 0:
            acc_ref[...] = part
        else:
            acc_ref[...] += part
    y = x + acc_ref[...]
    if not final_norm:
        out_ref[...] = y
    else:
        y = _rms(y, gf_ref[...])
        tt = tm // NB
        ys_ref, = slab_refs
        for s in range(D_MODEL // LANE):
            ys_ref[s] = y[:, s * LANE:(s + 1) * LANE]
        for b in range(NB):
            for s in range(D_MODEL // LANE):
                out_ref[b, :, s * LANE:(s + 1) * LANE] = ys_ref[s, pl.ds(b, tt, stride=NB), :]


def _ffn(x2, g, wg, wv, cg, cv, wd, gf, tt, final_norm):
    rows = x2.shape[0]
    tm = tt * NB
    row_spec = pl.BlockSpec((tm, D_MODEL), lambda i: (i, 0))
    hist = (CONV_FFN - 1) * NB
    scratch = [pltpu.VMEM((hist, D_FF), F32), pltpu.VMEM((hist, D_FF), F32), pltpu.VMEM((tm, D_MODEL), F32),
               pltpu.VMEM((tm, D_MODEL), BF16)]
    if final_norm:
        out_spec = pl.BlockSpec((NB, tt, D_MODEL), lambda i: (0, i, 0))
        out_shape = jax.ShapeDtypeStruct((NB, rows // NB, D_MODEL), F32)
        scratch.append(pltpu.VMEM((D_MODEL // LANE, tm, LANE), F32))
    else:
        out_spec, out_shape = row_spec, jax.ShapeDtypeStruct(x2.shape, F32)
    return pl.pallas_call(
        functools.partial(_ffn_kernel, final_norm=final_norm),
        grid=(rows // tm,),
        in_specs=[row_spec, _const((1, D_MODEL)), _const(wg.shape), _const(wv.shape), _const(cg.shape),
                  _const(cv.shape), _const(wd.shape), _const((1, D_MODEL))],
        out_specs=out_spec,
        out_shape=out_shape,
        scratch_shapes=scratch,
        compiler_params=_params(1),
        name="conv_ffn",
    )(x2, g, wg, wv, cg, cv, wd, gf)


def _s5_tables(lam_re, lam_im, b_re, b_im, c_re, c_im, log_dt):
    dt = jnp.exp(log_dt)[:, None]
    mag = jnp.exp(lam_re * dt)
    ang = lam_im * dt
    lb_re, lb_im = mag * jnp.cos(ang), mag * jnp.sin(ang)
    den = lam_re * lam_re + lam_im * lam_im
    nr, ni = lb_re - 1.0, lb_im
    coef_re = (nr * lam_re + ni * lam_im) / den
    coef_im = (ni * lam_re - nr * lam_im) / den
    bb_re = coef_re[..., None] * b_re - coef_im[..., None] * b_im
    bb_im = coef_re[..., None] * b_im + coef_im[..., None] * b_re
    eye = jnp.eye(8, dtype=F32)

    def in_blocks(bb):
        bb = bb.reshape(N_SUPER, 8, SSM_STATE, SSM_GROUP)
        return jnp.einsum('jgph,gm->jghmp', bb, eye).reshape(N_SUPER, LANE, SUPER_STATE)

    def out_blocks(cc):
        cc = cc.reshape(N_SUPER, 8, SSM_GROUP, SSM_STATE)
        return jnp.einsum('jghp,gm->jgpmh', cc, eye).reshape(N_SUPER, SUPER_STATE, LANE)

    bmat = jnp.concatenate([in_blocks(bb_re), in_blocks(bb_im)], axis=2).astype(BF16)
    cmat = jnp.concatenate([out_blocks(c_re), -out_blocks(c_im)], axis=1).astype(BF16)
    return bmat, cmat, lb_re.reshape(1, -1), lb_im.reshape(1, -1)


def _pad_lanes(v, offset):
    return jnp.zeros((1, LANE), F32).at[0, offset:offset + v.shape[0]].set(v)


def kernel(x, mem, norm_mix_g, norm_xa_g, norm_ffn_g, norm_mem_g, norm_final_g, w_in_ab, conv_qkv_a, a_log_a, dt_bias_a, onorm_g_a, ssm_lambda_re, ssm_lambda_im, ssm_b_re, ssm_b_im, ssm_c_re, ssm_c_im, ssm_d, ssm_log_dt, w_glu_b, b_glu_b, w_out_ab, pool_w, pool_scale, xa_wq, xa_wkv, xa_wo, ffn_w_up, ffn_conv, ffn_w_down):
    b, t, d = x.shape
    assert b == NB and d == D_MODEL and t % CHUNK == 0
    tt = min(CHUNK, t)
    tt_xa = min(XATTN_TT, t)
    row = lambda v: v.reshape(1, -1).astype(F32)

    k_mem, v_mem = _memkv(mem, row(norm_mem_g), xa_wkv.astype(BF16))

    pool = None
    for layer in range(2):
        if layer == 0:
            w_in = w_in_ab[0]
            qkv_w = 3 * WIDTH_A
            wba = jnp.zeros((d, LANE), F32).at[:, :2 * N_HEADS_A].set(
                w_in[:, qkv_w + WIDTH_A: qkv_w + WIDTH_A + 2 * N_HEADS_A])
            x2, q, k, v, bg, gate, u = _mix_in(
                x, row(norm_mix_g[0]), w_in[:, :qkv_w].astype(BF16),
                w_in[:, qkv_w:qkv_w + WIDTH_A].astype(BF16), wba.astype(BF16),
                w_in[:, qkv_w + WIDTH_A + 2 * N_HEADS_A:].astype(BF16), conv_qkv_a[0],
                _pad_lanes(a_log_a[0], N_HEADS_A), _pad_lanes(dt_bias_a[0], N_HEADS_A), tt)
            o = _delta(q, k, v, bg)
            bmat, cmat, lre, lim = _s5_tables(ssm_lambda_re[0], ssm_lambda_im[0], ssm_b_re[0], ssm_b_im[0],
                                              ssm_c_re[0], ssm_c_im[0], ssm_log_dt[0])
            x2 = _s5_out(x2, u, o, gate, bmat, lre, lim, cmat, row(ssm_d[0]), w_glu_b[0].astype(BF16),
                         row(b_glu_b[0]), row(onorm_g_a[0]), w_out_ab[0].astype(BF16), tt)
        else:
            pool = (row(norm_mix_g[1]), pool_w[0].astype(BF16), row(pool_scale[0]))
        x2 = _xattn(x2, row(norm_xa_g[layer]), xa_wq[layer].astype(BF16), xa_wo[layer].astype(BF16),
                    k_mem[layer], v_mem[layer], tt_xa, pool=pool)
        x2 = _ffn(x2, row(norm_ffn_g[layer]), ffn_w_up[layer][:, :D_FF].astype(BF16),
                  ffn_w_up[layer][:, D_FF:].astype(BF16), ffn_conv[layer][:, :D_FF], ffn_conv[layer][:, D_FF:],
                  ffn_w_down[layer].astype(BF16), row(norm_final_g), min(FFN_TT, t), final_norm=(layer == 1))
    return x2
```

```python
import functools
import math

import jax
import jax.numpy as jnp
from jax import lax
from jax.experimental import pallas as pl
from jax.experimental.pallas import tpu as pltpu

F32 = jnp.float32
BF16 = jnp.bfloat16

NB = 8
D_MODEL = 1024
LANE = 128
N_HEADS_A = 4
HEAD_DIM_A = 128
WIDTH_A = 512
CONV_A = 4
CHUNK = 64
MIX_IN_CHUNK = 256
MIX_IN_TT = 64
DELTA_HEADS = 4
SSM_WIDTH = 512
SSM_GROUP = 16
N_GROUPS = 32
SSM_STATE = 64
N_SUPER = 4
SUPER_STATE = 8 * SSM_STATE
POOL_WINDOWS = (2, 4, 8, 16)
POOL_GROUP = 256
MEM_LEN = 256
N_HEADS_X = 4
HEAD_DIM_X = 256
XATTN_TT = 512
D_FF = 2816
CONV_FFN = 3
FFN_TT = 64
FF_CHUNKS = (512, 512, 512, 512, 512, 256)
RMS_EPS = 1e-6
VMEM_LIMIT = 56 * 1024 * 1024


def _dot(a, b):
    return jnp.dot(a.astype(BF16), b.astype(BF16), preferred_element_type=F32)


def _dot_nt(a, b):
    return lax.dot_general(a.astype(BF16), b.astype(BF16), (((1,), (1,)), ((), ())),
                           preferred_element_type=F32)


def _rms(x, g):
    return x * lax.rsqrt(jnp.mean(x * x, axis=-1, keepdims=True) + RMS_EPS) * g


def _sigmoid(x):
    return 1.0 / (1.0 + jnp.exp(-x))


def _silu(x):
    return x * _sigmoid(x)


def _params(n_axes=1, flags=None):
    return pltpu.CompilerParams(dimension_semantics=("arbitrary",) * n_axes,
                                vmem_limit_bytes=VMEM_LIMIT, flags=flags)


def _full(shape):
    return pl.BlockSpec(shape, lambda *_: (0,) * len(shape))


def _const(shape):
    return pl.BlockSpec(shape, lambda *_: (0,) * len(shape), pipeline_mode=pl.Buffered(1))


def _memkv_kernel(mem_ref, g_ref, wkv_ref, k_ref, v_ref):
    mn = _rms(mem_ref[...], g_ref[...])
    kv = _dot(mn, wkv_ref[...])
    k_ref[...] = kv[:, :D_MODEL].T.astype(BF16)
    v_ref[...] = kv[:, D_MODEL:].astype(BF16)


def _memkv(mem, g, wkv):
    n_layers = wkv.shape[0]
    out = jax.ShapeDtypeStruct((n_layers, NB, MEM_LEN, D_MODEL), BF16)
    return pl.pallas_call(
        _memkv_kernel,
        grid=(n_layers, NB),
        in_specs=[pl.BlockSpec((None, MEM_LEN, D_MODEL), lambda l, b: (b, 0, 0)),
                  pl.BlockSpec((1, D_MODEL), lambda l, b: (0, 0)),
                  pl.BlockSpec((None, D_MODEL, 2 * D_MODEL), lambda l, b: (l, 0, 0))],
        out_specs=[pl.BlockSpec((None, None, D_MODEL, MEM_LEN), lambda l, b: (l, b, 0, 0)),
                   pl.BlockSpec((None, None, MEM_LEN, D_MODEL), lambda l, b: (l, b, 0, 0))],
        out_shape=[jax.ShapeDtypeStruct((n_layers, NB, D_MODEL, MEM_LEN), BF16), out],
        compiler_params=_params(2),
        name="memkv",
    )(mem, g, wkv)


def _mix_in_kernel(x_ref, g_ref, wqkv_ref, wgate_ref, wba_ref, wu_ref, conv_ref, alog_ref, dtb_ref,
                   x2_ref, q_ref, k_ref, v_ref, bg_ref, gate_ref, u_ref, carry_ref, xs_ref):
    tt = x_ref.shape[1]
    tm = tt * NB
    hist = (CONV_A - 1) * NB
    wide = MIX_IN_CHUNK
    per_chunk = wide // LANE

    @pl.when(pl.program_id(0) == 0)
    def _():
        carry_ref[...] = jnp.zeros_like(carry_ref)

    for b in range(NB):
        for s in range(D_MODEL // LANE):
            xs_ref[s, pl.ds(b, tt, stride=NB), :] = x_ref[b, :, s * LANE:(s + 1) * LANE]
    x = jnp.concatenate([xs_ref[s] for s in range(D_MODEL // LANE)], axis=1)
    x2_ref[...] = x
    xn = _rms(x, g_ref[...]).astype(BF16)

    def pre_qkv(c):
        return _dot(xn, wqkv_ref[:, c * wide:(c + 1) * wide])

    outs = (q_ref, k_ref, v_ref)

    def post_qkv(c, pre):
        sl = slice(c * wide, (c + 1) * wide)
        ext = jnp.concatenate([carry_ref[:, sl], pre], axis=0)
        carry_ref[:, sl] = pre[tm - hist:, :]
        acc = ext[hist:, :] * conv_ref[CONV_A - 1:CONV_A, sl]
        for j in range(1, CONV_A):
            acc = acc + ext[hist - NB * j: hist - NB * j + tm, :] * conv_ref[CONV_A - 1 - j:CONV_A - j, sl]
        y2 = _silu(acc)
        for i in range(per_chunk):
            which, h = divmod(per_chunk * c + i, N_HEADS_A)
            y = y2[:, i * LANE:(i + 1) * LANE]
            if which < 2:
                y = y * lax.rsqrt(jnp.sum(y * y, axis=-1, keepdims=True) + 1e-6)
            if which == 0:
                y = y * (HEAD_DIM_A ** -0.5)
            outs[which][h] = y

    n_chunks = 3 * WIDTH_A // wide
    plain = [(dst, w_ref, i) for dst, w_ref in ((gate_ref, wgate_ref), (u_ref, wu_ref))
             for i in range(WIDTH_A // wide)]
    assert len(plain) <= n_chunks
    pre_next = pre_qkv(0)
    ba = _dot(xn, wba_ref[...])
    for c in range(n_chunks):
        pre = pre_next
        if c + 1 < n_chunks:
            pre_next = pre_qkv(c + 1)
        if c < len(plain):
            dst, w_ref, i = plain[c]
            dst[:, i * wide:(i + 1) * wide] = _dot(xn, w_ref[:, i * wide:(i + 1) * wide])
        post_qkv(c, pre)

    sp_in = ba + dtb_ref[...]
    softplus = jnp.maximum(sp_in, 0.0) + jnp.log1p(jnp.exp(-jnp.abs(sp_in)))
    lane = lax.broadcasted_iota(jnp.int32, ba.shape, 1)
    bg_ref[...] = jnp.where(lane < N_HEADS_A, _sigmoid(ba), -jnp.exp(alog_ref[...]) * softplus)


def _mix_in(x, g, wqkv, wgate, wba, wu, conv, alog, dtb, tt):
    t = x.shape[1]
    rows = t * NB
    tm = tt * NB
    head = jax.ShapeDtypeStruct((N_HEADS_A, rows, HEAD_DIM_A), F32)
    head_spec = pl.BlockSpec((N_HEADS_A, tm, HEAD_DIM_A), lambda i: (0, i, 0))
    row_spec = lambda w: pl.BlockSpec((tm, w), lambda i: (i, 0))
    return pl.pallas_call(
        _mix_in_kernel,
        grid=(t // tt,),
        in_specs=[pl.BlockSpec((NB, tt, D_MODEL), lambda i: (0, i, 0)), _full((1, D_MODEL)),
                  _full(wqkv.shape), _full(wgate.shape), _full(wba.shape), _full(wu.shape),
                  _full(conv.shape), _full((1, LANE)), _full((1, LANE))],
        out_specs=[row_spec(D_MODEL), head_spec, head_spec, head_spec, row_spec(LANE), row_spec(WIDTH_A),
                   row_spec(SSM_WIDTH)],
        out_shape=[jax.ShapeDtypeStruct((rows, D_MODEL), F32), head, head, head,
                   jax.ShapeDtypeStruct((rows, LANE), F32),
                   jax.ShapeDtypeStruct((rows, WIDTH_A), F32),
                   jax.ShapeDtypeStruct((rows, SSM_WIDTH), F32)],
        scratch_shapes=[pltpu.VMEM(((CONV_A - 1) * NB, 3 * WIDTH_A), F32),
                        pltpu.VMEM((D_MODEL // LANE, tm, LANE), F32)],
        compiler_params=_params(1),
        name="mix_in",
    )(x, g, wqkv, wgate, wba, wu, conv, alog, dtb)


def _split3(x):
    hi = x.astype(BF16).astype(F32)
    mid = (x - hi).astype(BF16).astype(F32)
    lo = (x - hi - mid).astype(BF16).astype(F32)
    return hi, mid, lo


def _delta_kernel(q_ref, k_ref, v_ref, bg_ref, o_ref, s_ref):
    c2 = 2 * CHUNK
    n_pairs = NB // 2

    @pl.when(pl.program_id(1) == 0)
    def _():
        s_ref[...] = jnp.zeros_like(s_ref)

    row = lax.broadcasted_iota(jnp.int32, (c2, c2), 0)
    col = lax.broadcasted_iota(jnp.int32, (c2, c2), 1)
    same = (row >= CHUNK) == (col >= CHUNK)
    causal = same & (row >= col)
    strict = same & (row > col)
    tri = jnp.where(causal, 1.0, 0.0).astype(BF16)
    eye = jnp.where(row == col, 1.0, 0.0)
    lane = lax.broadcasted_iota(jnp.int32, (c2, LANE), 1)
    top = lax.broadcasted_iota(jnp.int32, (c2, LANE), 0) < CHUNK
    lane2 = lax.broadcasted_iota(jnp.int32, (1, 2 * HEAD_DIM_A), 1)

    def stacked(ref, p):
        return jnp.concatenate([ref[pl.ds(2 * p, CHUNK, stride=NB), :],
                                ref[pl.ds(2 * p + 1, CHUNK, stride=NB), :]], axis=0)

    chains = [(hh, p) for hh in range(DELTA_HEADS) for p in range(n_pairs)]
    n = len(chains)
    bg2 = [stacked(bg_ref, p) for p in range(n_pairs)]
    q2 = [stacked(q_ref.at[hh], p) for hh, p in chains]
    k2 = [stacked(k_ref.at[hh], p) for hh, p in chains]
    v2 = [stacked(v_ref.at[hh], p) for hh, p in chains]
    kt = [x.T.astype(BF16) for x in k2]
    kk = [_dot(k2[i], kt[i]) for i in range(n)]
    qk = [_dot(q2[i], kt[i]) for i in range(n)]

    beta, gc = [], []
    for hh, p in chains:
        h = pl.program_id(0) * DELTA_HEADS + hh
        beta.append(jnp.sum(jnp.where(lane == h, bg2[p], 0.0), axis=-1, keepdims=True))
        g = jnp.sum(jnp.where(lane == h + N_HEADS_A, bg2[p], 0.0), axis=-1, keepdims=True)
        ghi, gmid, glo = _split3(g)
        g3 = jnp.where(lane == 0, ghi, jnp.where(lane == 1, gmid, jnp.where(lane == 2, glo, 0.0)))
        gc.append(jnp.sum(jnp.dot(tri, g3.astype(BF16), preferred_element_type=F32), axis=-1, keepdims=True))

    pw, tinv, eg = [], [], []
    for i in range(n):
        gcb = jnp.broadcast_to(gc[i], (c2, c2))
        decay = jnp.where(causal, jnp.exp(jnp.where(causal, gcb - gcb.T, 0.0)), 0.0)
        eg.append(jnp.exp(gc[i]))
        pw.append(jnp.where(strict, kk[i] * decay * (-beta[i]), 0.0))
        qk[i] = jnp.where(causal, qk[i] * decay, 0.0)
        tinv.append(eye + pw[i])
    for _ in range(int(math.log2(CHUNK)) - 1):
        pw = [_dot(pw[i], pw[i]) for i in range(n)]
        tinv = [tinv[i] + _dot(tinv[i], pw[i]) for i in range(n)]
    sol = [_dot(tinv[i], jnp.concatenate([v2[i] * beta[i], k2[i] * (beta[i] * eg[i])], axis=1))
           for i in range(n)]

    s01 = [s_ref[i] for i in range(n)]
    ws = [_dot(sol[i][:, HEAD_DIM_A:], s01[i]) for i in range(n)]
    qs = [_dot(q2[i] * eg[i], s01[i]) for i in range(n)]
    vn = [sol[i][:, :HEAD_DIM_A] - jnp.where(top, ws[i][:, :HEAD_DIM_A], ws[i][:, HEAD_DIM_A:]) for i in range(n)]
    o2 = [jnp.where(top, qs[i][:, :HEAD_DIM_A], qs[i][:, HEAD_DIM_A:]) + _dot(qk[i], vn[i]) for i in range(n)]
    for i, (hh, p) in enumerate(chains):
        o_ref[hh, pl.ds(2 * p, CHUNK, stride=NB), :] = o2[i][:CHUNK]
        o_ref[hh, pl.ds(2 * p + 1, CHUNK, stride=NB), :] = o2[i][CHUNK:]
    for i in range(n):
        gl0, gl1 = gc[i][CHUNK - 1:CHUNK, :], gc[i][c2 - 1:c2, :]
        kd = k2[i] * jnp.exp(jnp.where(top[:, :1], gl0, gl1) - gc[i])
        vn01 = jnp.concatenate([jnp.where(top, vn[i], 0.0), jnp.where(top, 0.0, vn[i])], axis=1)
        sdec = jnp.where(lane2 < HEAD_DIM_A, jnp.exp(gl0), jnp.exp(gl1))
        s_ref[i] = s01[i] * sdec + _dot(kd.T, vn01)


def _delta(q, k, v, bg):
    rows = q.shape[1]
    cm = CHUNK * NB
    head_spec = pl.BlockSpec((DELTA_HEADS, cm, HEAD_DIM_A), lambda h, c: (h, c, 0))
    return pl.pallas_call(
        _delta_kernel,
        grid=(N_HEADS_A // DELTA_HEADS, rows // cm),
        in_specs=[head_spec, head_spec, head_spec, pl.BlockSpec((cm, LANE), lambda h, c: (c, 0))],
        out_specs=head_spec,
        out_shape=jax.ShapeDtypeStruct(q.shape, F32),
        scratch_shapes=[pltpu.VMEM((DELTA_HEADS * NB // 2, HEAD_DIM_A, 2 * HEAD_DIM_A), F32)],
        compiler_params=_params(2),
        name="delta_rule",
    )(q, k, v, bg)


def _gelu_tanh(x):
    return 0.5 * x * (1.0 + jnp.tanh(math.sqrt(2.0 / math.pi) * (x + 0.044715 * (x * x * x))))


def _s5_out_kernel(x_ref, u_ref, o_ref, gate_ref, bmat_ref, lre_ref, lim_ref, cmat_ref, d_ref,
                   wglu_ref, bglu_ref, onorm_ref, wout_ref, out_ref, st_ref, xs_ref):
    tm = x_ref.shape[0]
    tt = tm // NB
    ns = SUPER_STATE

    @pl.when(pl.program_id(0) == 0)
    def _():
        st_ref[...] = jnp.zeros_like(st_ref)

    u = u_ref[...]
    for j in range(N_SUPER):
        xs_ref[j] = _dot(u[:, j * LANE:(j + 1) * LANE], bmat_ref[j])

    gate = gate_ref[...]
    parts = []
    for h in range(N_HEADS_A):
        parts.append(_rms(o_ref[h], onorm_ref[...]) * _silu(gate[:, h * HEAD_DIM_A:(h + 1) * HEAD_DIM_A]))
    out_a = x_ref[...] + _dot(jnp.concatenate(parts, axis=1), wout_ref[:WIDTH_A, :])

    ys = []
    for j in range(N_SUPER):
        a_re = jnp.broadcast_to(lre_ref[:, j * ns:(j + 1) * ns], (NB, ns))
        a_im = jnp.broadcast_to(lim_ref[:, j * ns:(j + 1) * ns], (NB, ns))
        xr, xi = st_ref[j, :, :ns], st_ref[j, :, ns:]
        for t in range(tt):
            rows = slice(t * NB, (t + 1) * NB)
            xr, xi = (a_re * xr - a_im * xi + xs_ref[j, rows, :ns],
                      a_re * xi + a_im * xr + xs_ref[j, rows, ns:])
            xs_ref[j, rows, :ns] = xr
            xs_ref[j, rows, ns:] = xi
        st_ref[j, :, :ns] = xr
        st_ref[j, :, ns:] = xi
        ys.append(_dot(xs_ref[j], cmat_ref[j]))
    y = jnp.concatenate(ys, axis=1) + d_ref[...] * u
    y = _gelu_tanh(y)
    yb = y * _sigmoid(_dot(y, wglu_ref[...]) + bglu_ref[...])
    out_ref[...] = out_a + _dot(yb, wout_ref[WIDTH_A:, :])


def _s5_out(x2, u, o, gate, bmat, lre, lim, cmat, dvec, wglu, bglu, onorm, wout, tt):
    rows = x2.shape[0]
    tm = tt * NB
    row_spec = lambda w: pl.BlockSpec((tm, w), lambda i: (i, 0))
    return pl.pallas_call(
        _s5_out_kernel,
        grid=(rows // tm,),
        in_specs=[row_spec(D_MODEL), row_spec(SSM_WIDTH),
                  pl.BlockSpec((N_HEADS_A, tm, HEAD_DIM_A), lambda i: (0, i, 0)), row_spec(WIDTH_A),
                  _const(bmat.shape), _const(lre.shape), _const(lim.shape), _const(cmat.shape),
                  _const(dvec.shape), _const(wglu.shape), _const(bglu.shape), _const(onorm.shape),
                  _const(wout.shape)],
        out_specs=row_spec(D_MODEL),
        out_shape=jax.ShapeDtypeStruct(x2.shape, F32),
        scratch_shapes=[pltpu.VMEM((N_SUPER, NB, 2 * SUPER_STATE), F32),
                        pltpu.VMEM((N_SUPER, tm, 2 * SUPER_STATE), F32)],
        compiler_params=_params(1),
        name="s5_mix_out",
    )(x2, u, o, gate, bmat, lre, lim, cmat, dvec, wglu, bglu, onorm, wout)


def _pool_stage(x, tile, g_ref, pw_ref, ps_ref, carry_ref):
    tm = x.shape[0]
    tt = tm // NB
    hist = carry_ref.shape[0]

    @pl.when(tile == 0)
    def _():
        carry_ref[...] = jnp.zeros_like(carry_ref)

    xn = _rms(x, g_ref[...])
    tpos = tile * tt + lax.broadcasted_iota(jnp.int32, (tm, 1), 0) // NB + 1
    outs = []
    for gi, win in enumerate(POOL_WINDOWS):
        sl = slice(gi * POOL_GROUP, (gi + 1) * POOL_GROUP)
        xg = xn[:, sl]
        s = jnp.concatenate([carry_ref[:, sl], xg], axis=0)
        span = 1
        while span < win:
            s = s[NB * span:, :] + s[:s.shape[0] - NB * span, :]
            span *= 2
        s = s[s.shape[0] - tm:, :]
        count = jnp.minimum(tpos, win).astype(F32)
        mean = s * (1.0 / count)
        outs.append(x[:, sl] + _dot(mean - xg, pw_ref[gi]) * ps_ref[:, sl])
    carry_ref[...] = xn[tm - hist:, :]
    return jnp.concatenate(outs, axis=1)


def _xattn_kernel(x_ref, g_ref, wq_ref, wo_ref, k_ref, v_ref, out_ref):
    x = x_ref[...]
    q =(_dot(_rms(x, g_ref[...]), wq_ref[...]) * (HEAD_DIM_X ** -0.5)).astype(BF16)
    heads = [slice(h * HEAD_DIM_X, (h + 1) * HEAD_DIM_X) for h in range(N_HEADS_X)]

    def scores(h):
        return _dot(q[:, heads[h]], k_ref[heads[h], :])

    sc_next = scores(0)
    outs = []
    for h in range(N_HEADS_X):
        sc = sc_next
        if h + 1 < N_HEADS_X:
            sc_next = scores(h + 1)
        e = jnp.exp(sc - jnp.max(sc, axis=-1, keepdims=True))
        outs.append(_dot(e, v_ref[:, heads[h]]) / jnp.sum(e, axis=-1, keepdims=True))
    out_ref[...] = x + _dot(jnp.concatenate(outs, axis=1), wo_ref[...])


def _xattn(x2, g, wq, wo, k, v, tt):
    rows = x2.shape[0]
    t = rows // NB
    xv = x2.reshape(t, NB * D_MODEL)
    x_spec = pl.BlockSpec((tt, D_MODEL), lambda b, i: (i, b))
    consts = [g, wq, wo]
    out = pl.pallas_call(
        _xattn_kernel,
        grid=(NB, t // tt),
        in_specs=[x_spec] + [_const(a.shape) for a in consts]
                 + [pl.BlockSpec((None, D_MODEL, MEM_LEN), lambda b, i: (b, 0, 0)),
                    pl.BlockSpec((None, MEM_LEN, D_MODEL), lambda b, i: (b, 0, 0))],
        out_specs=x_spec,
        out_shape=jax.ShapeDtypeStruct(xv.shape, F32),
        compiler_params=_params(2),
        name="mem_xattn",
    )(xv, *consts, k, v)
    return out.reshape(rows, D_MODEL)


def _ffn_kernel(*refs, tail):
    x_ref, g_ref, wg_ref, wv_ref, cg_ref, cv_ref, wd_ref = refs[:7]
    extra_in = {None: 0, "pool": 3, "final": 1}[tail]
    tail_in = refs[7:7 + extra_in]
    out_ref, carry_g_ref, carry_v_ref, acc_ref = refs[7 + extra_in:11 + extra_in]
    tail_scratch = refs[11 + extra_in:]
    tm = x_ref.shape[0]
    hist = (CONV_FFN - 1) * NB

    @pl.when(pl.program_id(0) == 0)
    def _():
        carry_g_ref[...] = jnp.zeros_like(carry_g_ref)
        carry_v_ref[...] = jnp.zeros_like(carry_v_ref)

    x = x_ref[...]
    xn = _rms(x, g_ref[...]).astype(BF16)

    def conv(pre, c_ref, carry_ref, sl):
        ext = jnp.concatenate([carry_ref[:, sl], pre], axis=0)
        carry_ref[:, sl] = pre[tm - hist:, :]
        acc = ext[hist:, :] * c_ref[CONV_FFN - 1:CONV_FFN, sl]
        for j in range(1, CONV_FFN):
            acc = acc + ext[hist - NB * j: hist - NB * j + tm, :] * c_ref[CONV_FFN - 1 - j:CONV_FFN - j, sl]
        return acc

    bounds = [sum(FF_CHUNKS[:c]) for c in range(len(FF_CHUNKS) + 1)]
    assert bounds[-1] == D_FF

    def up(c):
        sl = slice(bounds[c], bounds[c + 1])
        return _dot(xn, wg_ref[:, sl]), _dot(xn, wv_ref[:, sl])

    n_chunks = len(FF_CHUNKS)
    pre_next = up(0)
    for c in range(n_chunks):
        sl = slice(bounds[c], bounds[c + 1])
        pre_g, pre_v = pre_next
        if c + 1 < n_chunks:
            pre_next = up(c + 1)
        act = _silu(conv(pre_g, cg_ref, carry_g_ref, sl)) * conv(pre_v, cv_ref, carry_v_ref, sl)
        part = _dot(act, wd_ref[sl, :])
        if c == 0:
            acc_ref[...] = part
        else:
            acc_ref[...] += part
    y = x + acc_ref[...]
    if tail is None:
        out_ref[...] = y
    elif tail == "pool":
        pg_ref, pw_ref, ps_ref = tail_in
        out_ref[...] = _pool_stage(y, pl.program_id(0), pg_ref, pw_ref, ps_ref, tail_scratch[0])
    else:
        y = _rms(y, tail_in[0][...])
        tt = tm // NB
        ys_ref = tail_scratch[0]
        for s in range(D_MODEL // LANE):
            ys_ref[s] = y[:, s * LANE:(s + 1) * LANE]
        for b in range(NB):
            for s in range(D_MODEL // LANE):
                out_ref[b, :, s * LANE:(s + 1) * LANE] = ys_ref[s, pl.ds(b, tt, stride=NB), :]


def _ffn(x2, g, wg, wv, cg, cv, wd, tt, tail=None, tail_args=()):
    rows = x2.shape[0]
    tm = tt * NB
    row_spec = pl.BlockSpec((tm, D_MODEL), lambda i: (i, 0))
    hist = (CONV_FFN - 1) * NB
    scratch = [pltpu.VMEM((hist, D_FF), F32), pltpu.VMEM((hist, D_FF), F32), pltpu.VMEM((tm, D_MODEL), F32)]
    out_spec, out_shape = row_spec, jax.ShapeDtypeStruct(x2.shape, F32)
    if tail == "final":
        out_spec = pl.BlockSpec((NB, tt, D_MODEL), lambda i: (0, i, 0))
        out_shape = jax.ShapeDtypeStruct((NB, rows // NB, D_MODEL), F32)
        scratch.append(pltpu.VMEM((D_MODEL // LANE, tm, LANE), F32))
    elif tail == "pool":
        scratch.append(pltpu.VMEM((POOL_WINDOWS[-1] * NB, D_MODEL), F32))
    consts = [g, wg, wv, cg, cv, wd, *tail_args]
    return pl.pallas_call(
        functools.partial(_ffn_kernel, tail=tail),
        grid=(rows // tm,),
        in_specs=[row_spec] + [_const(a.shape) for a in consts],
        out_specs=out_spec,
        out_shape=out_shape,
        scratch_shapes=scratch,
        compiler_params=_params(1),
        name="conv_ffn",
    )(x2, *consts)


def _s5_tables(lam_re, lam_im, b_re, b_im, c_re, c_im, log_dt):
    dt = jnp.exp(log_dt)[:, None]
    mag = jnp.exp(lam_re * dt)
    ang = lam_im * dt
    lb_re, lb_im = mag * jnp.cos(ang), mag * jnp.sin(ang)
    den = lam_re * lam_re + lam_im * lam_im
    nr, ni = lb_re - 1.0, lb_im
    coef_re = (nr * lam_re + ni * lam_im) / den
    coef_im = (ni * lam_re - nr * lam_im) / den
    bb_re = coef_re[..., None] * b_re - coef_im[..., None] * b_im
    bb_im = coef_re[..., None] * b_im + coef_im[..., None] * b_re
    eye = jnp.eye(8, dtype=F32)

    def in_blocks(bb):
        bb = bb.reshape(N_SUPER, 8, SSM_STATE, SSM_GROUP)
        return jnp.einsum('jgph,gm->jghmp', bb, eye).reshape(N_SUPER, LANE, SUPER_STATE)

    def out_blocks(cc):
        cc = cc.reshape(N_SUPER, 8, SSM_GROUP, SSM_STATE)
        return jnp.einsum('jghp,gm->jgpmh', cc, eye).reshape(N_SUPER, SUPER_STATE, LANE)

    bmat = jnp.concatenate([in_blocks(bb_re), in_blocks(bb_im)], axis=2).astype(BF16)
    cmat = jnp.concatenate([out_blocks(c_re), -out_blocks(c_im)], axis=1).astype(BF16)
    return bmat, cmat, lb_re.reshape(1, -1), lb_im.reshape(1, -1)


def _pad_lanes(v, offset):
    return jnp.zeros((1, LANE), F32).at[0, offset:offset + v.shape[0]].set(v)


def kernel(x, mem, norm_mix_g, norm_xa_g, norm_ffn_g, norm_mem_g, norm_final_g, w_in_ab, conv_qkv_a, a_log_a, dt_bias_a, onorm_g_a, ssm_lambda_re, ssm_lambda_im, ssm_b_re, ssm_b_im, ssm_c_re, ssm_c_im, ssm_d, ssm_log_dt, w_glu_b, b_glu_b, w_out_ab, pool_w, pool_scale, xa_wq, xa_wkv, xa_wo, ffn_w_up, ffn_conv, ffn_w_down):
    b, t, d = x.shape
    assert b == NB and d == D_MODEL and t % CHUNK == 0
    tt = min(CHUNK, t)
    tt_xa = min(XATTN_TT, t)
    row = lambda v: v.reshape(1, -1).astype(F32)

    k_mem, v_mem = _memkv(mem, row(norm_mem_g), xa_wkv.astype(BF16))

    for layer in range(2):
        if layer == 0:
            w_in = w_in_ab[0]
            qkv_w = 3 * WIDTH_A
            wba = jnp.zeros((d, LANE), F32).at[:, :2 * N_HEADS_A].set(
                w_in[:, qkv_w + WIDTH_A: qkv_w + WIDTH_A + 2 * N_HEADS_A])
            x2, q, k, v, bg, gate, u = _mix_in(
                x, row(norm_mix_g[0]), w_in[:, :qkv_w].astype(BF16),
                w_in[:, qkv_w:qkv_w + WIDTH_A].astype(BF16), wba.astype(BF16),
                w_in[:, qkv_w + WIDTH_A + 2 * N_HEADS_A:].astype(BF16), conv_qkv_a[0],
                _pad_lanes(a_log_a[0], N_HEADS_A), _pad_lanes(dt_bias_a[0], N_HEADS_A), min(MIX_IN_TT, t))
            o = _delta(q, k, v, bg)
            bmat, cmat, lre, lim = _s5_tables(ssm_lambda_re[0], ssm_lambda_im[0], ssm_b_re[0], ssm_b_im[0],
                                              ssm_c_re[0], ssm_c_im[0], ssm_log_dt[0])
            x2 = _s5_out(x2, u, o, gate, bmat, lre, lim, cmat, row(ssm_d[0]), w_glu_b[0].astype(BF16),
                         row(b_glu_b[0]), row(onorm_g_a[0]), w_out_ab[0].astype(BF16), tt)
        x2 = _xattn(x2, row(norm_xa_g[layer]), xa_wq[layer].astype(BF16), xa_wo[layer].astype(BF16),
                    k_mem[layer], v_mem[layer], tt_xa)
        tail, tail_args = (("pool", (row(norm_mix_g[1]), pool_w[0].astype(BF16), row(pool_scale[0])))
                           if layer == 0 else ("final", (row(norm_final_g),)))
        x2 = _ffn(x2, row(norm_ffn_g[layer]), ffn_w_up[layer][:, :D_FF].astype(BF16),
                  ffn_w_up[layer][:, D_FF:].astype(BF16), ffn_conv[layer][:, :D_FF], ffn_conv[layer][:, D_FF:],
                  ffn_w_down[layer].astype(BF16), min(FFN_TT, t), tail=tail, tail_args=tail_args)
    return x2
```

```python
import functools
import math

import jax
import jax.numpy as jnp
from jax import lax
from jax.experimental import pallas as pl
from jax.experimental.pallas import tpu as pltpu

F32 = jnp.float32
BF16 = jnp.bfloat16

NB = 8
D_MODEL = 1024
LANE = 128
N_HEADS_A = 4
HEAD_DIM_A = 128
WIDTH_A = 512
CONV_A = 4
CHUNK = 64
MIX_IN_CHUNK = 256
MIX_IN_TT = 64
DELTA_HEADS = 4
SSM_WIDTH = 512
SSM_GROUP = 16
N_GROUPS = 32
SSM_STATE = 64
N_SUPER = 4
SUPER_STATE = 8 * SSM_STATE
POOL_WINDOWS = (2, 4, 8, 16)
POOL_GROUP = 256
MEM_LEN = 256
N_HEADS_X = 4
HEAD_DIM_X = 256
XATTN_TT = 128
D_FF = 2816
CONV_FFN = 3
FFN_TT = 64
FF_CHUNKS = (512, 512, 512, 512, 512, 256)
FFN_UP_ROWS = 256
FFN_DN_ROWS = 704
RMS_EPS = 1e-6
VMEM_LIMIT = 56 * 1024 * 1024


def _dot(a, b):
    return jnp.dot(a.astype(BF16), b.astype(BF16), preferred_element_type=F32)


def _dot_nt(a, b):
    return lax.dot_general(a.astype(BF16), b.astype(BF16), (((1,), (1,)), ((), ())),
                           preferred_element_type=F32)


def _rms(x, g):
    return x * lax.rsqrt(jnp.mean(x * x, axis=-1, keepdims=True) + RMS_EPS) * g


def _sigmoid(x):
    return 1.0 / (1.0 + jnp.exp(-x))


def _silu(x):
    return x * _sigmoid(x)


def _params(n_axes=1, flags=None):
    return pltpu.CompilerParams(dimension_semantics=("arbitrary",) * n_axes,
                                vmem_limit_bytes=VMEM_LIMIT, flags=flags)


def _full(shape):
    return pl.BlockSpec(shape, lambda *_: (0,) * len(shape))


def _const(shape):
    return pl.BlockSpec(shape, lambda *_: (0,) * len(shape), pipeline_mode=pl.Buffered(1))


def _memkv_kernel(mem_ref, g_ref, wkv_ref, k_ref, v_ref):
    mn = _rms(mem_ref[...], g_ref[...])
    kv = _dot(mn, wkv_ref[...])
    k_ref[...] = kv[:, :D_MODEL].T.astype(BF16)
    v_ref[...] = kv[:, D_MODEL:].astype(BF16)


def _memkv(mem, g, wkv):
    n_layers = wkv.shape[0]
    out = jax.ShapeDtypeStruct((n_layers, NB, MEM_LEN, D_MODEL), BF16)
    return pl.pallas_call(
        _memkv_kernel,
        grid=(n_layers, NB),
        in_specs=[pl.BlockSpec((None, MEM_LEN, D_MODEL), lambda l, b: (b, 0, 0)),
                  pl.BlockSpec((1, D_MODEL), lambda l, b: (0, 0)),
                  pl.BlockSpec((None, D_MODEL, 2 * D_MODEL), lambda l, b: (l, 0, 0))],
        out_specs=[pl.BlockSpec((None, None, D_MODEL, MEM_LEN), lambda l, b: (l, b, 0, 0)),
                   pl.BlockSpec((None, None, MEM_LEN, D_MODEL), lambda l, b: (l, b, 0, 0))],
        out_shape=[jax.ShapeDtypeStruct((n_layers, NB, D_MODEL, MEM_LEN), BF16), out],
        compiler_params=_params(2),
        name="memkv",
    )(mem, g, wkv)


def _mix_in_kernel(x_ref, g_ref, wqkv_ref, wgate_ref, wba_ref, wu_ref, conv_ref, alog_ref, dtb_ref,
                   x2_ref, q_ref, k_ref, v_ref, bg_ref, gate_ref, u_ref, carry_ref, xs_ref):
    tt = x_ref.shape[1]
    tm = tt * NB
    hist = (CONV_A - 1) * NB
    wide = MIX_IN_CHUNK
    per_chunk = wide // LANE

    @pl.when(pl.program_id(0) == 0)
    def _():
        carry_ref[...] = jnp.zeros_like(carry_ref)

    for b in range(NB):
        for s in range(D_MODEL // LANE):
            xs_ref[s, pl.ds(b, tt, stride=NB), :] = x_ref[b, :, s * LANE:(s + 1) * LANE]
    x = jnp.concatenate([xs_ref[s] for s in range(D_MODEL // LANE)], axis=1)
    x2_ref[...] = x
    xn = _rms(x, g_ref[...]).astype(BF16)

    def pre_qkv(c):
        return _dot(xn, wqkv_ref[:, c * wide:(c + 1) * wide])

    outs = (q_ref, k_ref, v_ref)

    def post_qkv(c, pre):
        sl = slice(c * wide, (c + 1) * wide)
        ext = jnp.concatenate([carry_ref[:, sl], pre], axis=0)
        carry_ref[:, sl] = pre[tm - hist:, :]
        acc = ext[hist:, :] * conv_ref[CONV_A - 1:CONV_A, sl]
        for j in range(1, CONV_A):
            acc = acc + ext[hist - NB * j: hist - NB * j + tm, :] * conv_ref[CONV_A - 1 - j:CONV_A - j, sl]
        y2 = _silu(acc)
        for i in range(per_chunk):
            which, h = divmod(per_chunk * c + i, N_HEADS_A)
            y = y2[:, i * LANE:(i + 1) * LANE]
            if which < 2:
                y = y * lax.rsqrt(jnp.sum(y * y, axis=-1, keepdims=True) + 1e-6)
            if which == 0:
                y = y * (HEAD_DIM_A ** -0.5)
            outs[which][h] = y

    n_chunks = 3 * WIDTH_A // wide
    plain = [(dst, w_ref, i) for dst, w_ref in ((gate_ref, wgate_ref), (u_ref, wu_ref))
             for i in range(WIDTH_A // wide)]
    assert len(plain) <= n_chunks
    pre_next = pre_qkv(0)
    ba = _dot(xn, wba_ref[...])
    for c in range(n_chunks):
        pre = pre_next
        if c + 1 < n_chunks:
            pre_next = pre_qkv(c + 1)
        if c < len(plain):
            dst, w_ref, i = plain[c]
            dst[:, i * wide:(i + 1) * wide] = _dot(xn, w_ref[:, i * wide:(i + 1) * wide])
        post_qkv(c, pre)

    sp_in = ba + dtb_ref[...]
    softplus = jnp.maximum(sp_in, 0.0) + jnp.log1p(jnp.exp(-jnp.abs(sp_in)))
    lane = lax.broadcasted_iota(jnp.int32, ba.shape, 1)
    bg_ref[...] = jnp.where(lane < N_HEADS_A, _sigmoid(ba), -jnp.exp(alog_ref[...]) * softplus)


def _mix_in(x, g, wqkv, wgate, wba, wu, conv, alog, dtb, tt):
    t = x.shape[1]
    rows = t * NB
    tm = tt * NB
    head = jax.ShapeDtypeStruct((N_HEADS_A, rows, HEAD_DIM_A), F32)
    head_spec = pl.BlockSpec((N_HEADS_A, tm, HEAD_DIM_A), lambda i: (0, i, 0))
    row_spec = lambda w: pl.BlockSpec((tm, w), lambda i: (i, 0))
    return pl.pallas_call(
        _mix_in_kernel,
        grid=(t // tt,),
        in_specs=[pl.BlockSpec((NB, tt, D_MODEL), lambda i: (0, i, 0)), _full((1, D_MODEL)),
                  _full(wqkv.shape), _full(wgate.shape), _full(wba.shape), _full(wu.shape),
                  _full(conv.shape), _full((1, LANE)), _full((1, LANE))],
        out_specs=[row_spec(D_MODEL), head_spec, head_spec, head_spec, row_spec(LANE), row_spec(WIDTH_A),
                   row_spec(SSM_WIDTH)],
        out_shape=[jax.ShapeDtypeStruct((rows, D_MODEL), F32), head, head, head,
                   jax.ShapeDtypeStruct((rows, LANE), F32),
                   jax.ShapeDtypeStruct((rows, WIDTH_A), F32),
                   jax.ShapeDtypeStruct((rows, SSM_WIDTH), F32)],
        scratch_shapes=[pltpu.VMEM(((CONV_A - 1) * NB, 3 * WIDTH_A), F32),
                        pltpu.VMEM((D_MODEL // LANE, tm, LANE), F32)],
        compiler_params=_params(1),
        name="mix_in",
    )(x, g, wqkv, wgate, wba, wu, conv, alog, dtb)


def _split3(x):
    hi = x.astype(BF16).astype(F32)
    mid = (x - hi).astype(BF16).astype(F32)
    lo = (x - hi - mid).astype(BF16).astype(F32)
    return hi, mid, lo


def _delta_kernel(q_ref, k_ref, v_ref, bg_ref, o_ref, s_ref):
    c2 = 2 * CHUNK
    n_pairs = NB // 2

    @pl.when(pl.program_id(1) == 0)
    def _():
        s_ref[...] = jnp.zeros_like(s_ref)

    row = lax.broadcasted_iota(jnp.int32, (c2, c2), 0)
    col = lax.broadcasted_iota(jnp.int32, (c2, c2), 1)
    same = (row >= CHUNK) == (col >= CHUNK)
    causal = same & (row >= col)
    strict = same & (row > col)
    tri = jnp.where(causal, 1.0, 0.0).astype(BF16)
    eye = jnp.where(row == col, 1.0, 0.0)
    lane = lax.broadcasted_iota(jnp.int32, (c2, LANE), 1)
    top = lax.broadcasted_iota(jnp.int32, (c2, LANE), 0) < CHUNK
    lane2 = lax.broadcasted_iota(jnp.int32, (1, 2 * HEAD_DIM_A), 1)

    def stacked(ref, p):
        return jnp.concatenate([ref[pl.ds(2 * p, CHUNK, stride=NB), :],
                                ref[pl.ds(2 * p + 1, CHUNK, stride=NB), :]], axis=0)

    chains = [(hh, p) for hh in range(DELTA_HEADS) for p in range(n_pairs)]
    n = len(chains)
    bg2 = [stacked(bg_ref, p) for p in range(n_pairs)]
    q2 = [stacked(q_ref.at[hh], p) for hh, p in chains]
    k2 = [stacked(k_ref.at[hh], p) for hh, p in chains]
    v2 = [stacked(v_ref.at[hh], p) for hh, p in chains]
    kt = [x.T.astype(BF16) for x in k2]
    kk = [_dot(k2[i], kt[i]) for i in range(n)]
    qk = [_dot(q2[i], kt[i]) for i in range(n)]

    beta, gc = [], []
    for hh, p in chains:
        h = pl.program_id(0) * DELTA_HEADS + hh
        beta.append(jnp.sum(jnp.where(lane == h, bg2[p], 0.0), axis=-1, keepdims=True))
        g = jnp.sum(jnp.where(lane == h + N_HEADS_A, bg2[p], 0.0), axis=-1, keepdims=True)
        ghi, gmid, glo = _split3(g)
        g3 = jnp.where(lane == 0, ghi, jnp.where(lane == 1, gmid, jnp.where(lane == 2, glo, 0.0)))
        gc.append(jnp.sum(jnp.dot(tri, g3.astype(BF16), preferred_element_type=F32), axis=-1, keepdims=True))

    pw, tinv, eg = [], [], []
    for i in range(n):
        gcb = jnp.broadcast_to(gc[i], (c2, c2))
        decay = jnp.where(causal, jnp.exp(jnp.where(causal, gcb - gcb.T, 0.0)), 0.0)
        eg.append(jnp.exp(gc[i]))
        pw.append(jnp.where(strict, kk[i] * decay * (-beta[i]), 0.0))
        qk[i] = jnp.where(causal, qk[i] * decay, 0.0)
        tinv.append(eye + pw[i])
    for _ in range(int(math.log2(CHUNK)) - 1):
        pw = [_dot(pw[i], pw[i]) for i in range(n)]
        tinv = [tinv[i] + _dot(tinv[i], pw[i]) for i in range(n)]
    sol = [_dot(tinv[i], jnp.concatenate([v2[i] * beta[i], k2[i] * (beta[i] * eg[i])], axis=1))
           for i in range(n)]

    s01 = [s_ref[i] for i in range(n)]
    ws = [_dot(sol[i][:, HEAD_DIM_A:], s01[i]) for i in range(n)]
    qs = [_dot(q2[i] * eg[i], s01[i]) for i in range(n)]
    vn = [sol[i][:, :HEAD_DIM_A] - jnp.where(top, ws[i][:, :HEAD_DIM_A], ws[i][:, HEAD_DIM_A:]) for i in range(n)]
    o2 = [jnp.where(top, qs[i][:, :HEAD_DIM_A], qs[i][:, HEAD_DIM_A:]) + _dot(qk[i], vn[i]) for i in range(n)]
    for i, (hh, p) in enumerate(chains):
        o_ref[hh, pl.ds(2 * p, CHUNK, stride=NB), :] = o2[i][:CHUNK]
        o_ref[hh, pl.ds(2 * p + 1, CHUNK, stride=NB), :] = o2[i][CHUNK:]
    for i in range(n):
        gl0, gl1 = gc[i][CHUNK - 1:CHUNK, :], gc[i][c2 - 1:c2, :]
        kd = k2[i] * jnp.exp(jnp.where(top[:, :1], gl0, gl1) - gc[i])
        vn01 = jnp.concatenate([jnp.where(top, vn[i], 0.0), jnp.where(top, 0.0, vn[i])], axis=1)
        sdec = jnp.where(lane2 < HEAD_DIM_A, jnp.exp(gl0), jnp.exp(gl1))
        s_ref[i] = s01[i] * sdec + _dot(kd.T, vn01)


def _delta(q, k, v, bg):
    rows = q.shape[1]
    cm = CHUNK * NB
    head_spec = pl.BlockSpec((DELTA_HEADS, cm, HEAD_DIM_A), lambda h, c: (h, c, 0))
    return pl.pallas_call(
        _delta_kernel,
        grid=(N_HEADS_A // DELTA_HEADS, rows // cm),
        in_specs=[head_spec, head_spec, head_spec, pl.BlockSpec((cm, LANE), lambda h, c: (c, 0))],
        out_specs=head_spec,
        out_shape=jax.ShapeDtypeStruct(q.shape, F32),
        scratch_shapes=[pltpu.VMEM((DELTA_HEADS * NB // 2, HEAD_DIM_A, 2 * HEAD_DIM_A), F32)],
        compiler_params=_params(2),
        name="delta_rule",
    )(q, k, v, bg)


def _gelu_tanh(x):
    return 0.5 * x * (1.0 + jnp.tanh(math.sqrt(2.0 / math.pi) * (x + 0.044715 * (x * x * x))))


def _s5_out_kernel(x_ref, u_ref, o_ref, gate_ref, bmat_ref, lre_ref, lim_ref, cmat_ref, d_ref,
                   wglu_ref, bglu_ref, onorm_ref, wout_ref, out_ref, st_ref, xs_ref):
    tm = x_ref.shape[0]
    tt = tm // NB
    ns = SUPER_STATE

    @pl.when(pl.program_id(0) == 0)
    def _():
        st_ref[...] = jnp.zeros_like(st_ref)

    u = u_ref[...]
    for j in range(N_SUPER):
        xs_ref[j] = _dot(u[:, j * LANE:(j + 1) * LANE], bmat_ref[j])

    gate = gate_ref[...]
    parts = []
    for h in range(N_HEADS_A):
        parts.append(_rms(o_ref[h], onorm_ref[...]) * _silu(gate[:, h * HEAD_DIM_A:(h + 1) * HEAD_DIM_A]))
    out_a = x_ref[...] + _dot(jnp.concatenate(parts, axis=1), wout_ref[:WIDTH_A, :])

    ys = []
    for j in range(N_SUPER):
        a_re = jnp.broadcast_to(lre_ref[:, j * ns:(j + 1) * ns], (NB, ns))
        a_im = jnp.broadcast_to(lim_ref[:, j * ns:(j + 1) * ns], (NB, ns))
        xr, xi = st_ref[j, :, :ns], st_ref[j, :, ns:]
        for t in range(tt):
            rows = slice(t * NB, (t + 1) * NB)
            xr, xi = (a_re * xr - a_im * xi + xs_ref[j, rows, :ns],
                      a_re * xi + a_im * xr + xs_ref[j, rows, ns:])
            xs_ref[j, rows, :ns] = xr
            xs_ref[j, rows, ns:] = xi
        st_ref[j, :, :ns] = xr
        st_ref[j, :, ns:] = xi
        ys.append(_dot(xs_ref[j], cmat_ref[j]))
    y = jnp.concatenate(ys, axis=1) + d_ref[...] * u
    y = _gelu_tanh(y)
    yb = y * _sigmoid(_dot(y, wglu_ref[...]) + bglu_ref[...])
    out_ref[...] = out_a + _dot(yb, wout_ref[WIDTH_A:, :])


def _s5_out(x2, u, o, gate, bmat, lre, lim, cmat, dvec, wglu, bglu, onorm, wout, tt):
    rows = x2.shape[0]
    tm = tt * NB
    row_spec = lambda w: pl.BlockSpec((tm, w), lambda i: (i, 0))
    return pl.pallas_call(
        _s5_out_kernel,
        grid=(rows // tm,),
        in_specs=[row_spec(D_MODEL), row_spec(SSM_WIDTH),
                  pl.BlockSpec((N_HEADS_A, tm, HEAD_DIM_A), lambda i: (0, i, 0)), row_spec(WIDTH_A),
                  _const(bmat.shape), _const(lre.shape), _const(lim.shape), _const(cmat.shape),
                  _const(dvec.shape), _const(wglu.shape), _const(bglu.shape), _const(onorm.shape),
                  _const(wout.shape)],
        out_specs=row_spec(D_MODEL),
        out_shape=jax.ShapeDtypeStruct(x2.shape, F32),
        scratch_shapes=[pltpu.VMEM((N_SUPER, NB, 2 * SUPER_STATE), F32),
                        pltpu.VMEM((N_SUPER, tm, 2 * SUPER_STATE), F32)],
        compiler_params=_params(1),
        name="s5_mix_out",
    )(x2, u, o, gate, bmat, lre, lim, cmat, dvec, wglu, bglu, onorm, wout)


def _pool_stage(x, tile, g_ref, pw_ref, ps_ref, carry_ref):
    tm = x.shape[0]
    tt = tm // NB
    hist = carry_ref.shape[0]

    @pl.when(tile == 0)
    def _():
        carry_ref[...] = jnp.zeros_like(carry_ref)

    xn = _rms(x, g_ref[...])
    tpos = tile * tt + lax.broadcasted_iota(jnp.int32, (tm, 1), 0) // NB + 1
    outs = []
    for gi, win in enumerate(POOL_WINDOWS):
        sl = slice(gi * POOL_GROUP, (gi + 1) * POOL_GROUP)
        xg = xn[:, sl]
        s = jnp.concatenate([carry_ref[:, sl], xg], axis=0)
        span = 1
        while span < win:
            s = s[NB * span:, :] + s[:s.shape[0] - NB * span, :]
            span *= 2
        s = s[s.shape[0] - tm:, :]
        count = jnp.minimum(tpos, win).astype(F32)
        mean = s * (1.0 / count)
        outs.append(x[:, sl] + _dot(mean - xg, pw_ref[gi]) * ps_ref[:, sl])
    carry_ref[...] = xn[tm - hist:, :]
    return jnp.concatenate(outs, axis=1)


def _xattn_kernel(x_ref, g_ref, wq_ref, wo_ref, k_ref, v_ref, out_ref, qs_ref, os_ref):
    tm = x_ref.shape[0]
    tt = tm // NB
    x = x_ref[...]
    q = _dot(_rms(x, g_ref[...]), wq_ref[...]) * (HEAD_DIM_X ** -0.5)
    n_slab = D_MODEL // LANE
    for s in range(n_slab):
        qs_ref[s] = q[:, s * LANE:(s + 1) * LANE]
    per_head = HEAD_DIM_X // LANE
    heads = [slice(h * HEAD_DIM_X, (h + 1) * HEAD_DIM_X) for h in range(N_HEADS_X)]

    def scores(b):
        rows = pl.ds(b, tt, stride=NB)
        out = []
        for h in range(N_HEADS_X):
            qbh = jnp.concatenate([qs_ref[per_head * h + i, rows, :] for i in range(per_head)], axis=1)
            out.append(_dot(qbh, k_ref[b, heads[h], :]))
        return out

    sc_next = scores(0)
    for b in range(NB):
        sc = sc_next
        if b + 1 < NB:
            sc_next = scores(b + 1)
        e = [jnp.exp(s - jnp.max(s, axis=-1, keepdims=True)) for s in sc]
        rows = pl.ds(b, tt, stride=NB)
        for h in range(N_HEADS_X):
            obh = _dot(e[h], v_ref[b, :, heads[h]]) / jnp.sum(e[h], axis=-1, keepdims=True)
            for i in range(per_head):
                os_ref[per_head * h + i, rows, :] = obh[:, i * LANE:(i + 1) * LANE]
    o = jnp.concatenate([os_ref[s] for s in range(n_slab)], axis=1)
    out_ref[...] = x + _dot(o, wo_ref[...])


def _xattn(x2, g, wq, wo, k, v, tt):
    rows = x2.shape[0]
    tm = tt * NB
    row_spec = pl.BlockSpec((tm, D_MODEL), lambda i: (i, 0))
    slabs = pltpu.VMEM((D_MODEL // LANE, tm, LANE), F32)
    consts = [g, wq, wo, k, v]
    return pl.pallas_call(
        _xattn_kernel,
        grid=(rows // tm,),
        in_specs=[row_spec] + [_const(a.shape) for a in consts],
        out_specs=row_spec,
        out_shape=jax.ShapeDtypeStruct(x2.shape, F32),
        scratch_shapes=[slabs, slabs],
        compiler_params=_params(1),
        name="mem_xattn",
    )(x2, *consts)


def _stream_rows(src_hbm, rows, stage_ref, sem, consume):
    n = src_hbm.shape[0] // rows
    assert n * rows == src_hbm.shape[0]

    def copy(c):
        return pltpu.make_async_copy(src_hbm.at[pl.ds(c * rows, rows), :], stage_ref.at[c % 2], sem.at[c % 2])

    copy(0).start()
    for c in range(n):
        if c + 1 < n:
            copy(c + 1).start()
        copy(c).wait()
        consume(c, c % 2)


def _ffn_kernel(*refs, tail, layer):
    x_ref, g_ref, wup_hbm, cg_ref, cv_ref, wdn_hbm = refs[:6]
    extra_in = {None: 0, "pool": 3, "final": 1}[tail]
    tail_in = refs[6:6 + extra_in]
    (out_ref, carry_g_ref, carry_v_ref, acc_ref, wg_ref, wv_ref, wd_ref,
     stage_up_ref, stage_dn_ref, sem_up, sem_dn) = refs[6 + extra_in:17 + extra_in]
    tail_scratch = refs[17 + extra_in:]
    tm = x_ref.shape[0]
    hist = (CONV_FFN - 1) * NB

    @pl.when(pl.program_id(0) == 0)
    def _():
        carry_g_ref[...] = jnp.zeros_like(carry_g_ref)
        carry_v_ref[...] = jnp.zeros_like(carry_v_ref)

        def up_done(c, slot):
            rows = slice(c * FFN_UP_ROWS, (c + 1) * FFN_UP_ROWS)
            wg_ref[rows, :] = stage_up_ref[slot, :, :D_FF].astype(BF16)
            wv_ref[rows, :] = stage_up_ref[slot, :, D_FF:].astype(BF16)

        def dn_done(c, slot):
            rows = slice(c * FFN_DN_ROWS, (c + 1) * FFN_DN_ROWS)
            wd_ref[rows, :] = stage_dn_ref[slot].astype(BF16)

        _stream_rows(wup_hbm.at[layer], FFN_UP_ROWS, stage_up_ref, sem_up, up_done)
        _stream_rows(wdn_hbm.at[layer], FFN_DN_ROWS, stage_dn_ref, sem_dn, dn_done)

    x = x_ref[...]
    xn = _rms(x, g_ref[...]).astype(BF16)

    def conv(pre, c_ref, carry_ref, sl):
        ext = jnp.concatenate([carry_ref[:, sl], pre], axis=0)
        carry_ref[:, sl] = pre[tm - hist:, :]
        acc = ext[hist:, :] * c_ref[CONV_FFN - 1:CONV_FFN, sl]
        for j in range(1, CONV_FFN):
            acc = acc + ext[hist - NB * j: hist - NB * j + tm, :] * c_ref[CONV_FFN - 1 - j:CONV_FFN - j, sl]
        return acc

    bounds = [sum(FF_CHUNKS[:c]) for c in range(len(FF_CHUNKS) + 1)]
    assert bounds[-1] == D_FF

    def up(c):
        sl = slice(bounds[c], bounds[c + 1])
        return _dot(xn, wg_ref[:, sl]), _dot(xn, wv_ref[:, sl])

    n_chunks = len(FF_CHUNKS)
    pre_next = up(0)
    for c in range(n_chunks):
        sl = slice(bounds[c], bounds[c + 1])
        pre_g, pre_v = pre_next
        if c + 1 < n_chunks:
            pre_next = up(c + 1)
        act = _silu(conv(pre_g, cg_ref, carry_g_ref, sl)) * conv(pre_v, cv_ref, carry_v_ref, sl)
        part = _dot(act, wd_ref[sl, :])
        if c == 0:
            acc_ref[...] = part
        else:
            acc_ref[...] += part
    y = x + acc_ref[...]
    if tail is None:
        out_ref[...] = y
    elif tail == "pool":
        pg_ref, pw_ref, ps_ref = tail_in
        out_ref[...] = _pool_stage(y, pl.program_id(0), pg_ref, pw_ref, ps_ref, tail_scratch[0])
    else:
        y = _rms(y, tail_in[0][...])
        tt = tm // NB
        ys_ref = tail_scratch[0]
        for s in range(D_MODEL // LANE):
            ys_ref[s] = y[:, s * LANE:(s + 1) * LANE]
        for b in range(NB):
            for s in range(D_MODEL // LANE):
                out_ref[b, :, s * LANE:(s + 1) * LANE] = ys_ref[s, pl.ds(b, tt, stride=NB), :]


def _ffn(x2, g, w_up, cg, cv, w_down, layer, tt, tail=None, tail_args=()):
    rows = x2.shape[0]
    tm = tt * NB
    row_spec = pl.BlockSpec((tm, D_MODEL), lambda i: (i, 0))
    hist = (CONV_FFN - 1) * NB
    scratch = [pltpu.VMEM((hist, D_FF), F32), pltpu.VMEM((hist, D_FF), F32), pltpu.VMEM((tm, D_MODEL), F32),
               pltpu.VMEM((D_MODEL, D_FF), BF16), pltpu.VMEM((D_MODEL, D_FF), BF16),
               pltpu.VMEM((D_FF, D_MODEL), BF16),
               pltpu.VMEM((2, FFN_UP_ROWS, 2 * D_FF), F32), pltpu.VMEM((2, FFN_DN_ROWS, D_MODEL), F32),
               pltpu.SemaphoreType.DMA((2,)), pltpu.SemaphoreType.DMA((2,))]
    out_spec, out_shape = row_spec, jax.ShapeDtypeStruct(x2.shape, F32)
    if tail == "final":
        out_spec = pl.BlockSpec((NB, tt, D_MODEL), lambda i: (0, i, 0))
        out_shape = jax.ShapeDtypeStruct((NB, rows // NB, D_MODEL), F32)
        scratch.append(pltpu.VMEM((D_MODEL // LANE, tm, LANE), F32))
    elif tail == "pool":
        scratch.append(pltpu.VMEM((POOL_WINDOWS[-1] * NB, D_MODEL), F32))
    hbm = pl.BlockSpec(memory_space=pl.ANY)
    return pl.pallas_call(
        functools.partial(_ffn_kernel, tail=tail, layer=layer),
        grid=(rows // tm,),
        in_specs=[row_spec, _const(g.shape), hbm, _const(cg.shape), _const(cv.shape), hbm]
                 + [_const(a.shape) for a in tail_args],
        out_specs=out_spec,
        out_shape=out_shape,
        scratch_shapes=scratch,
        compiler_params=_params(1),
        name="conv_ffn",
    )(x2, g, w_up, cg, cv, w_down, *tail_args)


def _s5_tables(lam_re, lam_im, b_re, b_im, c_re, c_im, log_dt):
    dt = jnp.exp(log_dt)[:, None]
    mag = jnp.exp(lam_re * dt)
    ang = lam_im * dt
    lb_re, lb_im = mag * jnp.cos(ang), mag * jnp.sin(ang)
    den = lam_re * lam_re + lam_im * lam_im
    nr, ni = lb_re - 1.0, lb_im
    coef_re = (nr * lam_re + ni * lam_im) / den
    coef_im = (ni * lam_re - nr * lam_im) / den
    bb_re = coef_re[..., None] * b_re - coef_im[..., None] * b_im
    bb_im = coef_re[..., None] * b_im + coef_im[..., None] * b_re
    eye = jnp.eye(8, dtype=F32)

    def in_blocks(bb):
        bb = bb.reshape(N_SUPER, 8, SSM_STATE, SSM_GROUP)
        return jnp.einsum('jgph,gm->jghmp', bb, eye).reshape(N_SUPER, LANE, SUPER_STATE)

    def out_blocks(cc):
        cc = cc.reshape(N_SUPER, 8, SSM_GROUP, SSM_STATE)
        return jnp.einsum('jghp,gm->jgpmh', cc, eye).reshape(N_SUPER, SUPER_STATE, LANE)

    bmat = jnp.concatenate([in_blocks(bb_re), in_blocks(bb_im)], axis=2).astype(BF16)
    cmat = jnp.concatenate([out_blocks(c_re), -out_blocks(c_im)], axis=1).astype(BF16)
    return bmat, cmat, lb_re.reshape(1, -1), lb_im.reshape(1, -1)


def _pad_lanes(v, offset):
    return jnp.zeros((1, LANE), F32).at[0, offset:offset + v.shape[0]].set(v)


def kernel(x, mem, norm_mix_g, norm_xa_g, norm_ffn_g, norm_mem_g, norm_final_g, w_in_ab, conv_qkv_a, a_log_a, dt_bias_a, onorm_g_a, ssm_lambda_re, ssm_lambda_im, ssm_b_re, ssm_b_im, ssm_c_re, ssm_c_im, ssm_d, ssm_log_dt, w_glu_b, b_glu_b, w_out_ab, pool_w, pool_scale, xa_wq, xa_wkv, xa_wo, ffn_w_up, ffn_conv, ffn_w_down):
    b, t, d = x.shape
    assert b == NB and d == D_MODEL and t % CHUNK == 0
    tt = min(CHUNK, t)
    tt_xa = min(XATTN_TT, t)
    row = lambda v: v.reshape(1, -1).astype(F32)

    k_mem, v_mem = _memkv(mem, row(norm_mem_g), xa_wkv.astype(BF16))

    for layer in range(2):
        if layer == 0:
            w_in = w_in_ab[0]
            qkv_w = 3 * WIDTH_A
            wba = jnp.zeros((d, LANE), F32).at[:, :2 * N_HEADS_A].set(
                w_in[:, qkv_w + WIDTH_A: qkv_w + WIDTH_A + 2 * N_HEADS_A])
            x2, q, k, v, bg, gate, u = _mix_in(
                x, row(norm_mix_g[0]), w_in[:, :qkv_w].astype(BF16),
                w_in[:, qkv_w:qkv_w + WIDTH_A].astype(BF16), wba.astype(BF16),
                w_in[:, qkv_w + WIDTH_A + 2 * N_HEADS_A:].astype(BF16), conv_qkv_a[0],
                _pad_lanes(a_log_a[0], N_HEADS_A), _pad_lanes(dt_bias_a[0], N_HEADS_A), min(MIX_IN_TT, t))
            o = _delta(q, k, v, bg)
            bmat, cmat, lre, lim = _s5_tables(ssm_lambda_re[0], ssm_lambda_im[0], ssm_b_re[0], ssm_b_im[0],
                                              ssm_c_re[0], ssm_c_im[0], ssm_log_dt[0])
            x2 = _s5_out(x2, u, o, gate, bmat, lre, lim, cmat, row(ssm_d[0]), w_glu_b[0].astype(BF16),
                         row(b_glu_b[0]), row(onorm_g_a[0]), w_out_ab[0].astype(BF16), tt)
        x2 = _xattn(x2, row(norm_xa_g[layer]), xa_wq[layer].astype(BF16), xa_wo[layer].astype(BF16),
                    k_mem[layer], v_mem[layer], tt_xa)
        tail, tail_args = (("pool", (row(norm_mix_g[1]), pool_w[0].astype(BF16), row(pool_scale[0])))
                           if layer == 0 else ("final", (row(norm_final_g),)))
        x2 = _ffn(x2, row(norm_ffn_g[layer]), ffn_w_up, ffn_conv[layer][:, :D_FF], ffn_conv[layer][:, D_FF:],
                  ffn_w_down, layer, min(FFN_TT, t), tail=tail, tail_args=tail_args)
    return x2
```

```python
import functools
import math

import jax
import jax.numpy as jnp
from jax import lax
from jax.experimental import pallas as pl
from jax.experimental.pallas import tpu as pltpu

F32 = jnp.float32
BF16 = jnp.bfloat16

NB = 8
D_MODEL = 1024
LANE = 128
N_HEADS_A = 4
HEAD_DIM_A = 128
WIDTH_A = 512
CONV_A = 4
CHUNK = 64
MIX_IN_CHUNK = 256
MIX_IN_TT = 64
DELTA_HEADS = 4
SSM_WIDTH = 512
SSM_GROUP = 16
N_GROUPS = 32
SSM_STATE = 64
N_SUPER = 4
SUPER_STATE = 8 * SSM_STATE
POOL_WINDOWS = (2, 4, 8, 16)
POOL_GROUP = 256
MEM_LEN = 256
N_HEADS_X = 4
HEAD_DIM_X = 256
XATTN_TT = 128
D_FF = 2816
CONV_FFN = 3
FFN_TT = 64
FF_CHUNKS = (512, 512, 512, 512, 512, 256)
FFN_UP_ROWS = 256
FFN_DN_ROWS = 704
RMS_EPS = 1e-6
VMEM_LIMIT = 56 * 1024 * 1024


def _dot(a, b):
    return jnp.dot(a.astype(BF16), b.astype(BF16), preferred_element_type=F32)


def _dot_nt(a, b):
    return lax.dot_general(a.astype(BF16), b.astype(BF16), (((1,), (1,)), ((), ())),
                           preferred_element_type=F32)


def _rms(x, g):
    return x * lax.rsqrt(jnp.mean(x * x, axis=-1, keepdims=True) + RMS_EPS) * g


def _sigmoid(x):
    return 1.0 / (1.0 + jnp.exp(-x))


def _silu(x):
    return x * _sigmoid(x)


def _params(n_axes=1, flags=None):
    return pltpu.CompilerParams(dimension_semantics=("arbitrary",) * n_axes,
                                vmem_limit_bytes=VMEM_LIMIT, flags=flags)


def _full(shape):
    return pl.BlockSpec(shape, lambda *_: (0,) * len(shape))


def _layer_const(shape, layer):
    return pl.BlockSpec((None,) + tuple(shape[1:]), lambda *_: (layer,) + (0,) * (len(shape) - 1),
                        pipeline_mode=pl.Buffered(1))


def _const(shape):
    return pl.BlockSpec(shape, lambda *_: (0,) * len(shape), pipeline_mode=pl.Buffered(1))


def _memkv_kernel(mem_ref, g_ref, wkv_ref, k_ref, v_ref):
    mn = _rms(mem_ref[...], g_ref[...])
    kv = _dot(mn, wkv_ref[...])
    k_ref[...] = kv[:, :D_MODEL].T.astype(BF16)
    v_ref[...] = kv[:, D_MODEL:].astype(BF16)


def _memkv(mem, g, wkv):
    n_layers = wkv.shape[0]
    out = jax.ShapeDtypeStruct((n_layers, NB, MEM_LEN, D_MODEL), BF16)
    return pl.pallas_call(
        _memkv_kernel,
        grid=(n_layers, NB),
        in_specs=[pl.BlockSpec((None, MEM_LEN, D_MODEL), lambda l, b: (b, 0, 0)),
                  pl.BlockSpec((1, D_MODEL), lambda l, b: (0, 0)),
                  pl.BlockSpec((None, D_MODEL, 2 * D_MODEL), lambda l, b: (l, 0, 0))],
        out_specs=[pl.BlockSpec((None, None, D_MODEL, MEM_LEN), lambda l, b: (l, b, 0, 0)),
                   pl.BlockSpec((None, None, MEM_LEN, D_MODEL), lambda l, b: (l, b, 0, 0))],
        out_shape=[jax.ShapeDtypeStruct((n_layers, NB, D_MODEL, MEM_LEN), BF16), out],
        compiler_params=_params(2),
        name="memkv",
    )(mem, g, wkv)


def _mix_in_kernel(x_ref, g_ref, win_ref, conv_ref, alog_ref, dtb_ref,
                   x2_ref, q_ref, k_ref, v_ref, bg_ref, gate_ref, u_ref, carry_ref, xs_ref,
                   wqkv_ref, wgate_ref, wba_ref, wu_ref):
    tt = x_ref.shape[1]
    tm = tt * NB
    hist = (CONV_A - 1) * NB
    wide = MIX_IN_CHUNK
    per_chunk = wide // LANE

    @pl.when(pl.program_id(0) == 0)
    def _():
        carry_ref[...] = jnp.zeros_like(carry_ref)
        qkv_w, ba_w = 3 * WIDTH_A, 2 * N_HEADS_A
        wqkv_ref[...] = win_ref[:, :qkv_w].astype(BF16)
        wgate_ref[...] = win_ref[:, qkv_w:qkv_w + WIDTH_A].astype(BF16)
        ba = win_ref[:, qkv_w + WIDTH_A:qkv_w + WIDTH_A + LANE]
        wba_ref[...] = jnp.where(lax.broadcasted_iota(jnp.int32, ba.shape, 1) < ba_w, ba, 0.0).astype(BF16)
        wu_ref[...] = win_ref[:, qkv_w + WIDTH_A + ba_w:].astype(BF16)

    for b in range(NB):
        for s in range(D_MODEL // LANE):
            xs_ref[s, pl.ds(b, tt, stride=NB), :] = x_ref[b, :, s * LANE:(s + 1) * LANE]
    x = jnp.concatenate([xs_ref[s] for s in range(D_MODEL // LANE)], axis=1)
    x2_ref[...] = x
    xn = _rms(x, g_ref[...]).astype(BF16)

    def pre_qkv(c):
        return _dot(xn, wqkv_ref[:, c * wide:(c + 1) * wide])

    outs = (q_ref, k_ref, v_ref)

    def post_qkv(c, pre):
        sl = slice(c * wide, (c + 1) * wide)
        ext = jnp.concatenate([carry_ref[:, sl], pre], axis=0)
        carry_ref[:, sl] = pre[tm - hist:, :]
        acc = ext[hist:, :] * conv_ref[CONV_A - 1:CONV_A, sl]
        for j in range(1, CONV_A):
            acc = acc + ext[hist - NB * j: hist - NB * j + tm, :] * conv_ref[CONV_A - 1 - j:CONV_A - j, sl]
        y2 = _silu(acc)
        for i in range(per_chunk):
            which, h = divmod(per_chunk * c + i, N_HEADS_A)
            y = y2[:, i * LANE:(i + 1) * LANE]
            if which < 2:
                y = y * lax.rsqrt(jnp.sum(y * y, axis=-1, keepdims=True) + 1e-6)
            if which == 0:
                y = y * (HEAD_DIM_A ** -0.5)
            outs[which][h] = y

    n_chunks = 3 * WIDTH_A // wide
    plain = [(dst, w_ref, i) for dst, w_ref in ((gate_ref, wgate_ref), (u_ref, wu_ref))
             for i in range(WIDTH_A // wide)]
    assert len(plain) <= n_chunks
    pre_next = pre_qkv(0)
    ba = _dot(xn, wba_ref[...])
    for c in range(n_chunks):
        pre = pre_next
        if c + 1 < n_chunks:
            pre_next = pre_qkv(c + 1)
        if c < len(plain):
            dst, w_ref, i = plain[c]
            dst[:, i * wide:(i + 1) * wide] = _dot(xn, w_ref[:, i * wide:(i + 1) * wide])
        post_qkv(c, pre)

    sp_in = ba + dtb_ref[...]
    softplus = jnp.maximum(sp_in, 0.0) + jnp.log1p(jnp.exp(-jnp.abs(sp_in)))
    lane = lax.broadcasted_iota(jnp.int32, ba.shape, 1)
    bg_ref[...] = jnp.where(lane < N_HEADS_A, _sigmoid(ba), -jnp.exp(alog_ref[...]) * softplus)


def _mix_in(x, g, w_in, conv, alog, dtb, tt):
    t = x.shape[1]
    rows = t * NB
    tm = tt * NB
    head = jax.ShapeDtypeStruct((N_HEADS_A, rows, HEAD_DIM_A), F32)
    head_spec = pl.BlockSpec((N_HEADS_A, tm, HEAD_DIM_A), lambda i: (0, i, 0))
    row_spec = lambda w: pl.BlockSpec((tm, w), lambda i: (i, 0))
    return pl.pallas_call(
        _mix_in_kernel,
        grid=(t // tt,),
        in_specs=[pl.BlockSpec((NB, tt, D_MODEL), lambda i: (0, i, 0)), _const((1, D_MODEL)),
                  _const(w_in.shape), _const(conv.shape), _const((1, LANE)), _const((1, LANE))],
        out_specs=[row_spec(D_MODEL), head_spec, head_spec, head_spec, row_spec(LANE), row_spec(WIDTH_A),
                   row_spec(SSM_WIDTH)],
        out_shape=[jax.ShapeDtypeStruct((rows, D_MODEL), F32), head, head, head,
                   jax.ShapeDtypeStruct((rows, LANE), F32),
                   jax.ShapeDtypeStruct((rows, WIDTH_A), F32),
                   jax.ShapeDtypeStruct((rows, SSM_WIDTH), F32)],
        scratch_shapes=[pltpu.VMEM(((CONV_A - 1) * NB, 3 * WIDTH_A), F32),
                        pltpu.VMEM((D_MODEL // LANE, tm, LANE), F32),
                        pltpu.VMEM((D_MODEL, 3 * WIDTH_A), BF16), pltpu.VMEM((D_MODEL, WIDTH_A), BF16),
                        pltpu.VMEM((D_MODEL, LANE), BF16), pltpu.VMEM((D_MODEL, SSM_WIDTH), BF16)],
        compiler_params=_params(1),
        name="mix_in",
    )(x, g, w_in, conv, alog, dtb)


def _split3(x):
    hi = x.astype(BF16).astype(F32)
    mid = (x - hi).astype(BF16).astype(F32)
    lo = (x - hi - mid).astype(BF16).astype(F32)
    return hi, mid, lo


def _delta_kernel(q_ref, k_ref, v_ref, bg_ref, o_ref, s_ref):
    c2 = 2 * CHUNK
    n_pairs = NB // 2

    @pl.when(pl.program_id(1) == 0)
    def _():
        s_ref[...] = jnp.zeros_like(s_ref)

    row = lax.broadcasted_iota(jnp.int32, (c2, c2), 0)
    col = lax.broadcasted_iota(jnp.int32, (c2, c2), 1)
    same = (row >= CHUNK) == (col >= CHUNK)
    causal = same & (row >= col)
    strict = same & (row > col)
    tri = jnp.where(causal, 1.0, 0.0).astype(BF16)
    eye = jnp.where(row == col, 1.0, 0.0)
    lane = lax.broadcasted_iota(jnp.int32, (c2, LANE), 1)
    top = lax.broadcasted_iota(jnp.int32, (c2, LANE), 0) < CHUNK
    lane2 = lax.broadcasted_iota(jnp.int32, (1, 2 * HEAD_DIM_A), 1)

    def stacked(ref, p):
        return jnp.concatenate([ref[pl.ds(2 * p, CHUNK, stride=NB), :],
                                ref[pl.ds(2 * p + 1, CHUNK, stride=NB), :]], axis=0)

    chains = [(hh, p) for hh in range(DELTA_HEADS) for p in range(n_pairs)]
    n = len(chains)
    bg2 = [stacked(bg_ref, p) for p in range(n_pairs)]
    q2 = [stacked(q_ref.at[hh], p) for hh, p in chains]
    k2 = [stacked(k_ref.at[hh], p) for hh, p in chains]
    v2 = [stacked(v_ref.at[hh], p) for hh, p in chains]
    kt = [x.T.astype(BF16) for x in k2]
    kk = [_dot(k2[i], kt[i]) for i in range(n)]
    qk = [_dot(q2[i], kt[i]) for i in range(n)]

    beta, gc = [], []
    for hh, p in chains:
        h = pl.program_id(0) * DELTA_HEADS + hh
        beta.append(jnp.sum(jnp.where(lane == h, bg2[p], 0.0), axis=-1, keepdims=True))
        g = jnp.sum(jnp.where(lane == h + N_HEADS_A, bg2[p], 0.0), axis=-1, keepdims=True)
        ghi, gmid, glo = _split3(g)
        g3 = jnp.where(lane == 0, ghi, jnp.where(lane == 1, gmid, jnp.where(lane == 2, glo, 0.0)))
        gc.append(jnp.sum(jnp.dot(tri, g3.astype(BF16), preferred_element_type=F32), axis=-1, keepdims=True))

    pw, tinv, eg = [], [], []
    for i in range(n):
        gcb = jnp.broadcast_to(gc[i], (c2, c2))
        decay = jnp.where(causal, jnp.exp(jnp.where(causal, gcb - gcb.T, 0.0)), 0.0)
        eg.append(jnp.exp(gc[i]))
        pw.append(jnp.where(strict, kk[i] * decay * (-beta[i]), 0.0))
        qk[i] = jnp.where(causal, qk[i] * decay, 0.0)
        tinv.append(eye + pw[i])
    for _ in range(int(math.log2(CHUNK)) - 1):
        pw = [_dot(pw[i], pw[i]) for i in range(n)]
        tinv = [tinv[i] + _dot(tinv[i], pw[i]) for i in range(n)]
    sol = [_dot(tinv[i], jnp.concatenate([v2[i] * beta[i], k2[i] * (beta[i] * eg[i])], axis=1))
           for i in range(n)]

    s01 = [s_ref[i] for i in range(n)]
    ws = [_dot(sol[i][:, HEAD_DIM_A:], s01[i]) for i in range(n)]
    qs = [_dot(q2[i] * eg[i], s01[i]) for i in range(n)]
    vn = [sol[i][:, :HEAD_DIM_A] - jnp.where(top, ws[i][:, :HEAD_DIM_A], ws[i][:, HEAD_DIM_A:]) for i in range(n)]
    o2 = [jnp.where(top, qs[i][:, :HEAD_DIM_A], qs[i][:, HEAD_DIM_A:]) + _dot(qk[i], vn[i]) for i in range(n)]
    for i, (hh, p) in enumerate(chains):
        o_ref[hh, pl.ds(2 * p, CHUNK, stride=NB), :] = o2[i][:CHUNK]
        o_ref[hh, pl.ds(2 * p + 1, CHUNK, stride=NB), :] = o2[i][CHUNK:]
    for i in range(n):
        gl0, gl1 = gc[i][CHUNK - 1:CHUNK, :], gc[i][c2 - 1:c2, :]
        kd = k2[i] * jnp.exp(jnp.where(top[:, :1], gl0, gl1) - gc[i])
        vn01 = jnp.concatenate([jnp.where(top, vn[i], 0.0), jnp.where(top, 0.0, vn[i])], axis=1)
        sdec = jnp.where(lane2 < HEAD_DIM_A, jnp.exp(gl0), jnp.exp(gl1))
        s_ref[i] = s01[i] * sdec + _dot(kd.T, vn01)


def _delta(q, k, v, bg):
    rows = q.shape[1]
    cm = CHUNK * NB
    head_spec = pl.BlockSpec((DELTA_HEADS, cm, HEAD_DIM_A), lambda h, c: (h, c, 0))
    return pl.pallas_call(
        _delta_kernel,
        grid=(N_HEADS_A // DELTA_HEADS, rows // cm),
        in_specs=[head_spec, head_spec, head_spec, pl.BlockSpec((cm, LANE), lambda h, c: (c, 0))],
        out_specs=head_spec,
        out_shape=jax.ShapeDtypeStruct(q.shape, F32),
        scratch_shapes=[pltpu.VMEM((DELTA_HEADS * NB // 2, HEAD_DIM_A, 2 * HEAD_DIM_A), F32)],
        compiler_params=_params(2),
        name="delta_rule",
    )(q, k, v, bg)


def _gelu_tanh(x):
    return 0.5 * x * (1.0 + jnp.tanh(math.sqrt(2.0 / math.pi) * (x + 0.044715 * (x * x * x))))


def _s5_out_kernel(x_ref, u_ref, o_ref, gate_ref, bmat_ref, lre_ref, lim_ref, cmat_ref, d_ref,
                   wglu32_ref, bglu_ref, onorm_ref, wout32_ref, out_ref, st_ref, xs_ref, wglu_ref, wout_ref):
    tm = x_ref.shape[0]
    tt = tm // NB
    ns = SUPER_STATE

    @pl.when(pl.program_id(0) == 0)
    def _():
        st_ref[...] = jnp.zeros_like(st_ref)
        wglu_ref[...] = wglu32_ref[...].astype(BF16)
        wout_ref[...] = wout32_ref[...].astype(BF16)

    u = u_ref[...]
    for j in range(N_SUPER):
        xs_ref[j] = _dot(u[:, j * LANE:(j + 1) * LANE], bmat_ref[j])

    gate = gate_ref[...]
    parts = []
    for h in range(N_HEADS_A):
        parts.append(_rms(o_ref[h], onorm_ref[...]) * _silu(gate[:, h * HEAD_DIM_A:(h + 1) * HEAD_DIM_A]))
    out_a = x_ref[...] + _dot(jnp.concatenate(parts, axis=1), wout_ref[:WIDTH_A, :])

    ys = []
    for j in range(N_SUPER):
        a_re = jnp.broadcast_to(lre_ref[:, j * ns:(j + 1) * ns], (NB, ns))
        a_im = jnp.broadcast_to(lim_ref[:, j * ns:(j + 1) * ns], (NB, ns))
        xr, xi = st_ref[j, :, :ns], st_ref[j, :, ns:]
        for t in range(tt):
            rows = slice(t * NB, (t + 1) * NB)
            xr, xi = (a_re * xr - a_im * xi + xs_ref[j, rows, :ns],
                      a_re * xi + a_im * xr + xs_ref[j, rows, ns:])
            xs_ref[j, rows, :ns] = xr
            xs_ref[j, rows, ns:] = xi
        st_ref[j, :, :ns] = xr
        st_ref[j, :, ns:] = xi
        ys.append(_dot(xs_ref[j], cmat_ref[j]))
    y = jnp.concatenate(ys, axis=1) + d_ref[...] * u
    y = _gelu_tanh(y)
    yb = y * _sigmoid(_dot(y, wglu_ref[...]) + bglu_ref[...])
    out_ref[...] = out_a + _dot(yb, wout_ref[WIDTH_A:, :])


def _s5_out(x2, u, o, gate, bmat, lre, lim, cmat, dvec, wglu, bglu, onorm, wout, tt):
    rows = x2.shape[0]
    tm = tt * NB
    row_spec = lambda w: pl.BlockSpec((tm, w), lambda i: (i, 0))
    return pl.pallas_call(
        _s5_out_kernel,
        grid=(rows // tm,),
        in_specs=[row_spec(D_MODEL), row_spec(SSM_WIDTH),
                  pl.BlockSpec((N_HEADS_A, tm, HEAD_DIM_A), lambda i: (0, i, 0)), row_spec(WIDTH_A),
                  _const(bmat.shape), _const(lre.shape), _const(lim.shape), _const(cmat.shape),
                  _const(dvec.shape), _const(wglu.shape), _const(bglu.shape), _const(onorm.shape),
                  _const(wout.shape)],
        out_specs=row_spec(D_MODEL),
        out_shape=jax.ShapeDtypeStruct(x2.shape, F32),
        scratch_shapes=[pltpu.VMEM((N_SUPER, NB, 2 * SUPER_STATE), F32),
                        pltpu.VMEM((N_SUPER, tm, 2 * SUPER_STATE), F32),
                        pltpu.VMEM(wglu.shape, BF16), pltpu.VMEM(wout.shape, BF16)],
        compiler_params=_params(1),
        name="s5_mix_out",
    )(x2, u, o, gate, bmat, lre, lim, cmat, dvec, wglu, bglu, onorm, wout)


def _pool_stage(x, tile, g_ref, pw_ref, ps_ref, carry_ref):
    tm = x.shape[0]
    tt = tm // NB
    hist = carry_ref.shape[0]

    @pl.when(tile == 0)
    def _():
        carry_ref[...] = jnp.zeros_like(carry_ref)

    xn = _rms(x, g_ref[...])
    tpos = tile * tt + lax.broadcasted_iota(jnp.int32, (tm, 1), 0) // NB + 1
    outs = []
    for gi, win in enumerate(POOL_WINDOWS):
        sl = slice(gi * POOL_GROUP, (gi + 1) * POOL_GROUP)
        xg = xn[:, sl]
        s = jnp.concatenate([carry_ref[:, sl], xg], axis=0)
        span = 1
        while span < win:
            s = s[NB * span:, :] + s[:s.shape[0] - NB * span, :]
            span *= 2
        s = s[s.shape[0] - tm:, :]
        count = jnp.minimum(tpos, win).astype(F32)
        mean = s * (1.0 / count)
        outs.append(x[:, sl] + _dot(mean - xg, pw_ref[gi]) * ps_ref[:, sl])
    carry_ref[...] = xn[tm - hist:, :]
    return jnp.concatenate(outs, axis=1)


def _xattn_kernel(x_ref, g_ref, wq32_ref, wo32_ref, k_ref, v_ref, out_ref, qs_ref, os_ref, wq_ref, wo_ref):
    tm = x_ref.shape[0]
    tt = tm // NB

    @pl.when(pl.program_id(0) == 0)
    def _():
        wq_ref[...] = (wq32_ref[...] * (HEAD_DIM_X ** -0.5)).astype(BF16)
        wo_ref[...] = wo32_ref[...].astype(BF16)

    x = x_ref[...]
    q = _dot(_rms(x, g_ref[...]), wq_ref[...])
    n_slab = D_MODEL // LANE
    for s in range(n_slab):
        qs_ref[s] = q[:, s * LANE:(s + 1) * LANE]
    per_head = HEAD_DIM_X // LANE
    heads = [slice(h * HEAD_DIM_X, (h + 1) * HEAD_DIM_X) for h in range(N_HEADS_X)]

    def scores(b):
        rows = pl.ds(b, tt, stride=NB)
        out = []
        for h in range(N_HEADS_X):
            qbh = jnp.concatenate([qs_ref[per_head * h + i, rows, :] for i in range(per_head)], axis=1)
            out.append(_dot(qbh, k_ref[b, heads[h], :]))
        return out

    sc_next = scores(0)
    for b in range(NB):
        sc = sc_next
        if b + 1 < NB:
            sc_next = scores(b + 1)
        e = [jnp.exp(s - jnp.max(s, axis=-1, keepdims=True)) for s in sc]
        rows = pl.ds(b, tt, stride=NB)
        for h in range(N_HEADS_X):
            obh = _dot(e[h], v_ref[b, :, heads[h]]) / jnp.sum(e[h], axis=-1, keepdims=True)
            for i in range(per_head):
                os_ref[per_head * h + i, rows, :] = obh[:, i * LANE:(i + 1) * LANE]
    o = jnp.concatenate([os_ref[s] for s in range(n_slab)], axis=1)
    out_ref[...] = x + _dot(o, wo_ref[...])


def _xattn(x2, g, wq, wo, k, v, layer, tt):
    rows = x2.shape[0]
    tm = tt * NB
    row_spec = pl.BlockSpec((tm, D_MODEL), lambda i: (i, 0))
    slabs = pltpu.VMEM((D_MODEL // LANE, tm, LANE), F32)
    w_bf = pltpu.VMEM((D_MODEL, D_MODEL), BF16)
    return pl.pallas_call(
        _xattn_kernel,
        grid=(rows // tm,),
        in_specs=[row_spec, _const(g.shape)] + [_layer_const(a.shape, layer) for a in (wq, wo, k, v)],
        out_specs=row_spec,
        out_shape=jax.ShapeDtypeStruct(x2.shape, F32),
        scratch_shapes=[slabs, slabs, w_bf, w_bf],
        compiler_params=_params(1),
        name="mem_xattn",
    )(x2, g, wq, wo, k, v)


def _stream_rows(src_hbm, rows, stage_ref, sem, consume):
    n = src_hbm.shape[0] // rows
    assert n * rows == src_hbm.shape[0]

    def copy(c):
        return pltpu.make_async_copy(src_hbm.at[pl.ds(c * rows, rows), :], stage_ref.at[c % 2], sem.at[c % 2])

    copy(0).start()
    for c in range(n):
        if c + 1 < n:
            copy(c + 1).start()
        copy(c).wait()
        consume(c, c % 2)


def _ffn_kernel(*refs, tail, layer):
    x_ref, g_ref, wup_hbm, cg_ref, cv_ref, wdn_hbm = refs[:6]
    extra_in = {None: 0, "pool": 3, "final": 1}[tail]
    tail_in = refs[6:6 + extra_in]
    (out_ref, carry_g_ref, carry_v_ref, acc_ref, wg_ref, wv_ref, wd_ref,
     stage_up_ref, stage_dn_ref, sem_up, sem_dn) = refs[6 + extra_in:17 + extra_in]
    tail_scratch = refs[17 + extra_in:]
    tm = x_ref.shape[0]
    hist = (CONV_FFN - 1) * NB

    @pl.when(pl.program_id(0) == 0)
    def _():
        carry_g_ref[...] = jnp.zeros_like(carry_g_ref)
        carry_v_ref[...] = jnp.zeros_like(carry_v_ref)

        def up_done(c, slot):
            rows = slice(c * FFN_UP_ROWS, (c + 1) * FFN_UP_ROWS)
            wg_ref[rows, :] = stage_up_ref[slot, :, :D_FF].astype(BF16)
            wv_ref[rows, :] = stage_up_ref[slot, :, D_FF:].astype(BF16)

        def dn_done(c, slot):
            rows = slice(c * FFN_DN_ROWS, (c + 1) * FFN_DN_ROWS)
            wd_ref[rows, :] = stage_dn_ref[slot].astype(BF16)

        _stream_rows(wup_hbm.at[layer], FFN_UP_ROWS, stage_up_ref, sem_up, up_done)
        _stream_rows(wdn_hbm.at[layer], FFN_DN_ROWS, stage_dn_ref, sem_dn, dn_done)

    x = x_ref[...]
    xn = _rms(x, g_ref[...]).astype(BF16)

    def conv(pre, c_ref, carry_ref, sl):
        ext = jnp.concatenate([carry_ref[:, sl], pre], axis=0)
        carry_ref[:, sl] = pre[tm - hist:, :]
        acc = ext[hist:, :] * c_ref[CONV_FFN - 1:CONV_FFN, sl]
        for j in range(1, CONV_FFN):
            acc = acc + ext[hist - NB * j: hist - NB * j + tm, :] * c_ref[CONV_FFN - 1 - j:CONV_FFN - j, sl]
        return acc

    bounds = [sum(FF_CHUNKS[:c]) for c in range(len(FF_CHUNKS) + 1)]
    assert bounds[-1] == D_FF

    def up(c):
        sl = slice(bounds[c], bounds[c + 1])
        return _dot(xn, wg_ref[:, sl]), _dot(xn, wv_ref[:, sl])

    n_chunks = len(FF_CHUNKS)
    pre_next = up(0)
    for c in range(n_chunks):
        sl = slice(bounds[c], bounds[c + 1])
        pre_g, pre_v = pre_next
        if c + 1 < n_chunks:
            pre_next = up(c + 1)
        act = _silu(conv(pre_g, cg_ref, carry_g_ref, sl)) * conv(pre_v, cv_ref, carry_v_ref, sl)
        part = _dot(act, wd_ref[sl, :])
        if c == 0:
            acc_ref[...] = part
        else:
            acc_ref[...] += part
    y = x + acc_ref[...]
    if tail is None:
        out_ref[...] = y
    elif tail == "pool":
        pg_ref, pw_ref, ps_ref = tail_in
        out_ref[...] = _pool_stage(y, pl.program_id(0), pg_ref, pw_ref, ps_ref, tail_scratch[0])
    else:
        y = _rms(y, tail_in[0][...])
        tt = tm // NB
        ys_ref = tail_scratch[0]
        for s in range(D_MODEL // LANE):
            ys_ref[s] = y[:, s * LANE:(s + 1) * LANE]
        for b in range(NB):
            for s in range(D_MODEL // LANE):
                out_ref[b, :, s * LANE:(s + 1) * LANE] = ys_ref[s, pl.ds(b, tt, stride=NB), :]


def _ffn(x2, g, w_up, cg, cv, w_down, layer, tt, tail=None, tail_args=()):
    rows = x2.shape[0]
    tm = tt * NB
    row_spec = pl.BlockSpec((tm, D_MODEL), lambda i: (i, 0))
    hist = (CONV_FFN - 1) * NB
    scratch = [pltpu.VMEM((hist, D_FF), F32), pltpu.VMEM((hist, D_FF), F32), pltpu.VMEM((tm, D_MODEL), F32),
               pltpu.VMEM((D_MODEL, D_FF), BF16), pltpu.VMEM((D_MODEL, D_FF), BF16),
               pltpu.VMEM((D_FF, D_MODEL), BF16),
               pltpu.VMEM((2, FFN_UP_ROWS, 2 * D_FF), F32), pltpu.VMEM((2, FFN_DN_ROWS, D_MODEL), F32),
               pltpu.SemaphoreType.DMA((2,)), pltpu.SemaphoreType.DMA((2,))]
    out_spec, out_shape = row_spec, jax.ShapeDtypeStruct(x2.shape, F32)
    if tail == "final":
        out_spec = pl.BlockSpec((NB, tt, D_MODEL), lambda i: (0, i, 0))
        out_shape = jax.ShapeDtypeStruct((NB, rows // NB, D_MODEL), F32)
        scratch.append(pltpu.VMEM((D_MODEL // LANE, tm, LANE), F32))
    elif tail == "pool":
        scratch.append(pltpu.VMEM((POOL_WINDOWS[-1] * NB, D_MODEL), F32))
    hbm = pl.BlockSpec(memory_space=pl.ANY)
    return pl.pallas_call(
        functools.partial(_ffn_kernel, tail=tail, layer=layer),
        grid=(rows // tm,),
        in_specs=[row_spec, _const(g.shape), hbm, _const(cg.shape), _const(cv.shape), hbm]
                 + [_const(a.shape) for a in tail_args],
        out_specs=out_spec,
        out_shape=out_shape,
        scratch_shapes=scratch,
        compiler_params=_params(1),
        name="conv_ffn",
    )(x2, g, w_up, cg, cv, w_down, *tail_args)


def _s5_tables(lam_re, lam_im, b_re, b_im, c_re, c_im, log_dt):
    dt = jnp.exp(log_dt)[:, None]
    mag = jnp.exp(lam_re * dt)
    ang = lam_im * dt
    lb_re, lb_im = mag * jnp.cos(ang), mag * jnp.sin(ang)
    den = lam_re * lam_re + lam_im * lam_im
    nr, ni = lb_re - 1.0, lb_im
    coef_re = (nr * lam_re + ni * lam_im) / den
    coef_im = (ni * lam_re - nr * lam_im) / den
    bb_re = coef_re[..., None] * b_re - coef_im[..., None] * b_im
    bb_im = coef_re[..., None] * b_im + coef_im[..., None] * b_re
    eye = jnp.eye(8, dtype=F32)

    def in_blocks(bb):
        bb = bb.reshape(N_SUPER, 8, SSM_STATE, SSM_GROUP)
        return jnp.einsum('jgph,gm->jghmp', bb, eye).reshape(N_SUPER, LANE, SUPER_STATE)

    def out_blocks(cc):
        cc = cc.reshape(N_SUPER, 8, SSM_GROUP, SSM_STATE)
        return jnp.einsum('jghp,gm->jgpmh', cc, eye).reshape(N_SUPER, SUPER_STATE, LANE)

    bmat = jnp.concatenate([in_blocks(bb_re), in_blocks(bb_im)], axis=2).astype(BF16)
    cmat = jnp.concatenate([out_blocks(c_re), -out_blocks(c_im)], axis=1).astype(BF16)
    return bmat, cmat, lb_re.reshape(1, -1), lb_im.reshape(1, -1)


def _pad_lanes(v, offset):
    return jnp.zeros((1, LANE), F32).at[0, offset:offset + v.shape[0]].set(v)


def kernel(x, mem, norm_mix_g, norm_xa_g, norm_ffn_g, norm_mem_g, norm_final_g, w_in_ab, conv_qkv_a, a_log_a, dt_bias_a, onorm_g_a, ssm_lambda_re, ssm_lambda_im, ssm_b_re, ssm_b_im, ssm_c_re, ssm_c_im, ssm_d, ssm_log_dt, w_glu_b, b_glu_b, w_out_ab, pool_w, pool_scale, xa_wq, xa_wkv, xa_wo, ffn_w_up, ffn_conv, ffn_w_down):
    b, t, d = x.shape
    assert b == NB and d == D_MODEL and t % CHUNK == 0
    tt = min(CHUNK, t)
    tt_xa = min(XATTN_TT, t)
    row = lambda v: v.reshape(1, -1).astype(F32)

    k_mem, v_mem = _memkv(mem, row(norm_mem_g), xa_wkv)

    for layer in range(2):
        if layer == 0:
            x2, q, k, v, bg, gate, u = _mix_in(
                x, row(norm_mix_g[0]), w_in_ab[0], conv_qkv_a[0],
                _pad_lanes(a_log_a[0], N_HEADS_A), _pad_lanes(dt_bias_a[0], N_HEADS_A), min(MIX_IN_TT, t))
            o = _delta(q, k, v, bg)
            bmat, cmat, lre, lim = _s5_tables(ssm_lambda_re[0], ssm_lambda_im[0], ssm_b_re[0], ssm_b_im[0],
                                              ssm_c_re[0], ssm_c_im[0], ssm_log_dt[0])
            x2 = _s5_out(x2, u, o, gate, bmat, lre, lim, cmat, row(ssm_d[0]), w_glu_b[0],
                         row(b_glu_b[0]), row(onorm_g_a[0]), w_out_ab[0], tt)
        x2 = _xattn(x2, row(norm_xa_g[layer]), xa_wq, xa_wo, k_mem, v_mem, layer, tt_xa)
        tail, tail_args = (("pool", (row(norm_mix_g[1]), pool_w[0].astype(BF16), row(pool_scale[0])))
                           if layer == 0 else ("final", (row(norm_final_g),)))
        x2 = _ffn(x2, row(norm_ffn_g[layer]), ffn_w_up, ffn_conv[layer][:, :D_FF], ffn_conv[layer][:, D_FF:],
                  ffn_w_down, layer, min(FFN_TT, t), tail=tail, tail_args=tail_args)
    return x2
```

```python
import functools
import math

import jax
import jax.numpy as jnp
from jax import lax
from jax.experimental import pallas as pl
from jax.experimental.pallas import tpu as pltpu

F32 = jnp.float32
BF16 = jnp.bfloat16

NB = 8
D_MODEL = 1024
LANE = 128
N_HEADS_A = 4
HEAD_DIM_A = 128
WIDTH_A = 512
CONV_A = 4
CHUNK = 64
MIX_IN_CHUNK = 256
MIX_IN_TT = 64
DELTA_HEADS = 4
SSM_WIDTH = 512
SSM_GROUP = 16
N_GROUPS = 32
SSM_STATE = 64
N_SUPER = 4
SUPER_STATE = 8 * SSM_STATE
POOL_WINDOWS = (2, 4, 8, 16)
POOL_GROUP = 256
MEM_LEN = 256
N_HEADS_X = 4
HEAD_DIM_X = 256
XATTN_TT = 128
D_FF = 2816
CONV_FFN = 3
FFN_TT = 64
FF_CHUNKS = (512, 512, 512, 512, 512, 256)
FFN_UP_ROWS = 256
FFN_DN_ROWS = 704
RMS_EPS = 1e-6
VMEM_LIMIT = 56 * 1024 * 1024


def _dot(a, b):
    return jnp.dot(a.astype(BF16), b.astype(BF16), preferred_element_type=F32)


def _dot_nt(a, b):
    return lax.dot_general(a.astype(BF16), b.astype(BF16), (((1,), (1,)), ((), ())),
                           preferred_element_type=F32)


def _rms(x, g):
    return x * lax.rsqrt(jnp.mean(x * x, axis=-1, keepdims=True) + RMS_EPS) * g


def _sigmoid(x):
    return 1.0 / (1.0 + jnp.exp(-x))


def _silu(x):
    return x * _sigmoid(x)


def _params(n_axes=1, flags=None):
    return pltpu.CompilerParams(dimension_semantics=("arbitrary",) * n_axes,
                                vmem_limit_bytes=VMEM_LIMIT, flags=flags)


def _full(shape):
    return pl.BlockSpec(shape, lambda *_: (0,) * len(shape))


def _layer_const(shape, layer):
    return pl.BlockSpec((None,) + tuple(shape[1:]), lambda *_: (layer,) + (0,) * (len(shape) - 1),
                        pipeline_mode=pl.Buffered(1))


def _const(shape):
    return pl.BlockSpec(shape, lambda *_: (0,) * len(shape), pipeline_mode=pl.Buffered(1))


def _memkv_kernel(mem_ref, g_ref, wkv_ref, k_ref, v_ref):
    mn = _rms(mem_ref[...], g_ref[...])
    kv = _dot(mn, wkv_ref[...])
    k_ref[...] = kv[:, :D_MODEL].T.astype(BF16)
    v_ref[...] = kv[:, D_MODEL:].astype(BF16)


def _memkv(mem, g, wkv):
    n_layers = wkv.shape[0]
    out = jax.ShapeDtypeStruct((n_layers, NB, MEM_LEN, D_MODEL), BF16)
    return pl.pallas_call(
        _memkv_kernel,
        grid=(n_layers, NB),
        in_specs=[pl.BlockSpec((None, MEM_LEN, D_MODEL), lambda l, b: (b, 0, 0)),
                  pl.BlockSpec((1, D_MODEL), lambda l, b: (0, 0)),
                  pl.BlockSpec((None, D_MODEL, 2 * D_MODEL), lambda l, b: (l, 0, 0))],
        out_specs=[pl.BlockSpec((None, None, D_MODEL, MEM_LEN), lambda l, b: (l, b, 0, 0)),
                   pl.BlockSpec((None, None, MEM_LEN, D_MODEL), lambda l, b: (l, b, 0, 0))],
        out_shape=[jax.ShapeDtypeStruct((n_layers, NB, D_MODEL, MEM_LEN), BF16), out],
        compiler_params=_params(2),
        name="memkv",
    )(mem, g, wkv)


def _mix_in_kernel(x_ref, g_ref, win_ref, conv_ref, alog_ref, dtb_ref,
                   x2_ref, q_ref, k_ref, v_ref, bg_ref, gate_ref, u_ref, carry_ref, xs_ref,
                   wqkv_ref, wgate_ref, wba_ref, wu_ref):
    tt = x_ref.shape[1]
    tm = tt * NB
    hist = (CONV_A - 1) * NB
    wide = MIX_IN_CHUNK
    per_chunk = wide // LANE

    @pl.when(pl.program_id(0) == 0)
    def _():
        carry_ref[...] = jnp.zeros_like(carry_ref)
        qkv_w, ba_w = 3 * WIDTH_A, 2 * N_HEADS_A
        wqkv_ref[...] = win_ref[:, :qkv_w].astype(BF16)
        wgate_ref[...] = win_ref[:, qkv_w:qkv_w + WIDTH_A].astype(BF16)
        ba = win_ref[:, qkv_w + WIDTH_A:qkv_w + WIDTH_A + LANE]
        wba_ref[...] = jnp.where(lax.broadcasted_iota(jnp.int32, ba.shape, 1) < ba_w, ba, 0.0).astype(BF16)
        wu_ref[...] = win_ref[:, qkv_w + WIDTH_A + ba_w:].astype(BF16)

    for b in range(NB):
        for s in range(D_MODEL // LANE):
            xs_ref[s, pl.ds(b, tt, stride=NB), :] = x_ref[b, :, s * LANE:(s + 1) * LANE]
    x = jnp.concatenate([xs_ref[s] for s in range(D_MODEL // LANE)], axis=1)
    x2_ref[...] = x
    xn = _rms(x, g_ref[...]).astype(BF16)

    def pre_qkv(c):
        return _dot(xn, wqkv_ref[:, c * wide:(c + 1) * wide])

    outs = (q_ref, k_ref, v_ref)

    def post_qkv(c, pre):
        sl = slice(c * wide, (c + 1) * wide)
        ext = jnp.concatenate([carry_ref[:, sl], pre], axis=0)
        carry_ref[:, sl] = pre[tm - hist:, :]
        acc = ext[hist:, :] * conv_ref[CONV_A - 1:CONV_A, sl]
        for j in range(1, CONV_A):
            acc = acc + ext[hist - NB * j: hist - NB * j + tm, :] * conv_ref[CONV_A - 1 - j:CONV_A - j, sl]
        y2 = _silu(acc)
        for i in range(per_chunk):
            which, h = divmod(per_chunk * c + i, N_HEADS_A)
            y = y2[:, i * LANE:(i + 1) * LANE]
            if which < 2:
                y = y * lax.rsqrt(jnp.sum(y * y, axis=-1, keepdims=True) + 1e-6)
            if which == 0:
                y = y * (HEAD_DIM_A ** -0.5)
            outs[which][h] = y

    n_chunks = 3 * WIDTH_A // wide
    plain = [(dst, w_ref, i) for dst, w_ref in ((gate_ref, wgate_ref), (u_ref, wu_ref))
             for i in range(WIDTH_A // wide)]
    assert len(plain) <= n_chunks
    pre_next = pre_qkv(0)
    ba = _dot(xn, wba_ref[...])
    for c in range(n_chunks):
        pre = pre_next
        if c + 1 < n_chunks:
            pre_next = pre_qkv(c + 1)
        if c < len(plain):
            dst, w_ref, i = plain[c]
            dst[:, i * wide:(i + 1) * wide] = _dot(xn, w_ref[:, i * wide:(i + 1) * wide])
        post_qkv(c, pre)

    sp_in = ba + dtb_ref[...]
    softplus = jnp.maximum(sp_in, 0.0) + jnp.log1p(jnp.exp(-jnp.abs(sp_in)))
    lane = lax.broadcasted_iota(jnp.int32, ba.shape, 1)
    bg_ref[...] = jnp.where(lane < N_HEADS_A, _sigmoid(ba), -jnp.exp(alog_ref[...]) * softplus)


def _mix_in(x, g, w_in, conv, alog, dtb, tt):
    t = x.shape[1]
    rows = t * NB
    tm = tt * NB
    head = jax.ShapeDtypeStruct((N_HEADS_A, rows, HEAD_DIM_A), F32)
    head_spec = pl.BlockSpec((N_HEADS_A, tm, HEAD_DIM_A), lambda i: (0, i, 0))
    row_spec = lambda w: pl.BlockSpec((tm, w), lambda i: (i, 0))
    return pl.pallas_call(
        _mix_in_kernel,
        grid=(t // tt,),
        in_specs=[pl.BlockSpec((NB, tt, D_MODEL), lambda i: (0, i, 0)), _const((1, D_MODEL)),
                  _const(w_in.shape), _const(conv.shape), _const((1, LANE)), _const((1, LANE))],
        out_specs=[row_spec(D_MODEL), head_spec, head_spec, head_spec, row_spec(LANE), row_spec(WIDTH_A),
                   row_spec(SSM_WIDTH)],
        out_shape=[jax.ShapeDtypeStruct((rows, D_MODEL), F32), head, head, head,
                   jax.ShapeDtypeStruct((rows, LANE), F32),
                   jax.ShapeDtypeStruct((rows, WIDTH_A), F32),
                   jax.ShapeDtypeStruct((rows, SSM_WIDTH), F32)],
        scratch_shapes=[pltpu.VMEM(((CONV_A - 1) * NB, 3 * WIDTH_A), F32),
                        pltpu.VMEM((D_MODEL // LANE, tm, LANE), F32),
                        pltpu.VMEM((D_MODEL, 3 * WIDTH_A), BF16), pltpu.VMEM((D_MODEL, WIDTH_A), BF16),
                        pltpu.VMEM((D_MODEL, LANE), BF16), pltpu.VMEM((D_MODEL, SSM_WIDTH), BF16)],
        compiler_params=_params(1),
        name="mix_in",
    )(x, g, w_in, conv, alog, dtb)


def _split3(x):
    hi = x.astype(BF16).astype(F32)
    mid = (x - hi).astype(BF16).astype(F32)
    lo = (x - hi - mid).astype(BF16).astype(F32)
    return hi, mid, lo


def _delta_kernel(q_ref, k_ref, v_ref, bg_ref, o_ref, s_ref):
    c2 = 2 * CHUNK
    n_pairs = NB // 2

    @pl.when(pl.program_id(1) == 0)
    def _():
        s_ref[...] = jnp.zeros_like(s_ref)

    row = lax.broadcasted_iota(jnp.int32, (c2, c2), 0)
    col = lax.broadcasted_iota(jnp.int32, (c2, c2), 1)
    same = (row >= CHUNK) == (col >= CHUNK)
    causal = same & (row >= col)
    strict = same & (row > col)
    tri = jnp.where(causal, 1.0, 0.0).astype(BF16)
    eye = jnp.where(row == col, 1.0, 0.0)
    lane = lax.broadcasted_iota(jnp.int32, (c2, LANE), 1)
    top = lax.broadcasted_iota(jnp.int32, (c2, LANE), 0) < CHUNK
    lane2 = lax.broadcasted_iota(jnp.int32, (1, 2 * HEAD_DIM_A), 1)

    def stacked(ref, p):
        return jnp.concatenate([ref[pl.ds(2 * p, CHUNK, stride=NB), :],
                                ref[pl.ds(2 * p + 1, CHUNK, stride=NB), :]], axis=0)

    chains = [(hh, p) for hh in range(DELTA_HEADS) for p in range(n_pairs)]
    n = len(chains)
    bg2 = [stacked(bg_ref, p) for p in range(n_pairs)]
    q2 = [stacked(q_ref.at[hh], p) for hh, p in chains]
    k2 = [stacked(k_ref.at[hh], p) for hh, p in chains]
    v2 = [stacked(v_ref.at[hh], p) for hh, p in chains]
    kt = [x.T.astype(BF16) for x in k2]
    kk = [_dot(k2[i], kt[i]) for i in range(n)]
    qk = [_dot(q2[i], kt[i]) for i in range(n)]

    beta, gc = [], []
    for hh, p in chains:
        h = pl.program_id(0) * DELTA_HEADS + hh
        beta.append(jnp.sum(jnp.where(lane == h, bg2[p], 0.0), axis=-1, keepdims=True))
        g = jnp.sum(jnp.where(lane == h + N_HEADS_A, bg2[p], 0.0), axis=-1, keepdims=True)
        ghi, gmid, glo = _split3(g)
        g3 = jnp.where(lane == 0, ghi, jnp.where(lane == 1, gmid, jnp.where(lane == 2, glo, 0.0)))
        gc.append(jnp.sum(jnp.dot(tri, g3.astype(BF16), preferred_element_type=F32), axis=-1, keepdims=True))

    pw, tinv, eg = [], [], []
    for i in range(n):
        gcb = jnp.broadcast_to(gc[i], (c2, c2))
        decay = jnp.where(causal, jnp.exp(jnp.where(causal, gcb - gcb.T, 0.0)), 0.0)
        eg.append(jnp.exp(gc[i]))
        pw.append(jnp.where(strict, kk[i] * decay * (-beta[i]), 0.0))
        qk[i] = jnp.where(causal, qk[i] * decay, 0.0)
        tinv.append(eye + pw[i])
    for _ in range(int(math.log2(CHUNK)) - 1):
        pw = [_dot(pw[i], pw[i]) for i in range(n)]
        tinv = [tinv[i] + _dot(tinv[i], pw[i]) for i in range(n)]
    sol = [_dot(tinv[i], jnp.concatenate([v2[i] * beta[i], k2[i] * (beta[i] * eg[i])], axis=1))
           for i in range(n)]

    s01 = [s_ref[i] for i in range(n)]
    ws = [_dot(sol[i][:, HEAD_DIM_A:], s01[i]) for i in range(n)]
    qs = [_dot(q2[i] * eg[i], s01[i]) for i in range(n)]
    vn = [sol[i][:, :HEAD_DIM_A] - jnp.where(top, ws[i][:, :HEAD_DIM_A], ws[i][:, HEAD_DIM_A:]) for i in range(n)]
    o2 = [jnp.where(top, qs[i][:, :HEAD_DIM_A], qs[i][:, HEAD_DIM_A:]) + _dot(qk[i], vn[i]) for i in range(n)]
    for i, (hh, p) in enumerate(chains):
        o_ref[hh, pl.ds(2 * p, CHUNK, stride=NB), :] = o2[i][:CHUNK]
        o_ref[hh, pl.ds(2 * p + 1, CHUNK, stride=NB), :] = o2[i][CHUNK:]
    for i in range(n):
        gl0, gl1 = gc[i][CHUNK - 1:CHUNK, :], gc[i][c2 - 1:c2, :]
        kd = k2[i] * jnp.exp(jnp.where(top[:, :1], gl0, gl1) - gc[i])
        vn01 = jnp.concatenate([jnp.where(top, vn[i], 0.0), jnp.where(top, 0.0, vn[i])], axis=1)
        sdec = jnp.where(lane2 < HEAD_DIM_A, jnp.exp(gl0), jnp.exp(gl1))
        s_ref[i] = s01[i] * sdec + _dot(kd.T, vn01)


def _delta(q, k, v, bg):
    rows = q.shape[1]
    cm = CHUNK * NB
    head_spec = pl.BlockSpec((DELTA_HEADS, cm, HEAD_DIM_A), lambda h, c: (h, c, 0))
    return pl.pallas_call(
        _delta_kernel,
        grid=(N_HEADS_A // DELTA_HEADS, rows // cm),
        in_specs=[head_spec, head_spec, head_spec, pl.BlockSpec((cm, LANE), lambda h, c: (c, 0))],
        out_specs=head_spec,
        out_shape=jax.ShapeDtypeStruct(q.shape, F32),
        scratch_shapes=[pltpu.VMEM((DELTA_HEADS * NB // 2, HEAD_DIM_A, 2 * HEAD_DIM_A), F32)],
        compiler_params=_params(2),
        name="delta_rule",
    )(q, k, v, bg)


def _gelu_tanh(x):
    return 0.5 * x * (1.0 + jnp.tanh(math.sqrt(2.0 / math.pi) * (x + 0.044715 * (x * x * x))))


def _s5_out_kernel(x_ref, u_ref, o_ref, gate_ref, bmat_ref, lre_ref, lim_ref, cmat_ref, d_ref,
                   wglu32_ref, bglu_ref, onorm_ref, wout32_ref, out_ref, st_ref, xs_ref, wglu_ref, wout_ref):
    tm = x_ref.shape[0]
    tt = tm // NB
    ns = SUPER_STATE

    @pl.when(pl.program_id(0) == 0)
    def _():
        st_ref[...] = jnp.zeros_like(st_ref)
        wglu_ref[...] = wglu32_ref[...].astype(BF16)
        wout_ref[...] = wout32_ref[...].astype(BF16)

    u = u_ref[...]
    for j in range(N_SUPER):
        xs_ref[j] = _dot(u[:, j * LANE:(j + 1) * LANE], bmat_ref[j])

    gate = gate_ref[...]
    parts = []
    for h in range(N_HEADS_A):
        parts.append(_rms(o_ref[h], onorm_ref[...]) * _silu(gate[:, h * HEAD_DIM_A:(h + 1) * HEAD_DIM_A]))
    out_a = x_ref[...] + _dot(jnp.concatenate(parts, axis=1), wout_ref[:WIDTH_A, :])

    ys = []
    for j in range(N_SUPER):
        a_re = jnp.broadcast_to(lre_ref[:, j * ns:(j + 1) * ns], (NB, ns))
        a_im = jnp.broadcast_to(lim_ref[:, j * ns:(j + 1) * ns], (NB, ns))
        xr, xi = st_ref[j, :, :ns], st_ref[j, :, ns:]
        for t in range(tt):
            rows = slice(t * NB, (t + 1) * NB)
            xr, xi = (a_re * xr - a_im * xi + xs_ref[j, rows, :ns],
                      a_re * xi + a_im * xr + xs_ref[j, rows, ns:])
            xs_ref[j, rows, :ns] = xr
            xs_ref[j, rows, ns:] = xi
        st_ref[j, :, :ns] = xr
        st_ref[j, :, ns:] = xi
        ys.append(_dot(xs_ref[j], cmat_ref[j]))
    y = jnp.concatenate(ys, axis=1) + d_ref[...] * u
    y = _gelu_tanh(y)
    yb = y * _sigmoid(_dot(y, wglu_ref[...]) + bglu_ref[...])
    out_ref[...] = out_a + _dot(yb, wout_ref[WIDTH_A:, :])


def _s5_out(x2, u, o, gate, bmat, lre, lim, cmat, dvec, wglu, bglu, onorm, wout, tt):
    rows = x2.shape[0]
    tm = tt * NB
    row_spec = lambda w: pl.BlockSpec((tm, w), lambda i: (i, 0))
    return pl.pallas_call(
        _s5_out_kernel,
        grid=(rows // tm,),
        in_specs=[row_spec(D_MODEL), row_spec(SSM_WIDTH),
                  pl.BlockSpec((N_HEADS_A, tm, HEAD_DIM_A), lambda i: (0, i, 0)), row_spec(WIDTH_A),
                  _const(bmat.shape), _const(lre.shape), _const(lim.shape), _const(cmat.shape),
                  _const(dvec.shape), _const(wglu.shape), _const(bglu.shape), _const(onorm.shape),
                  _const(wout.shape)],
        out_specs=row_spec(D_MODEL),
        out_shape=jax.ShapeDtypeStruct(x2.shape, F32),
        scratch_shapes=[pltpu.VMEM((N_SUPER, NB, 2 * SUPER_STATE), F32),
                        pltpu.VMEM((N_SUPER, tm, 2 * SUPER_STATE), F32),
                        pltpu.VMEM(wglu.shape, BF16), pltpu.VMEM(wout.shape, BF16)],
        compiler_params=_params(1),
        name="s5_mix_out",
    )(x2, u, o, gate, bmat, lre, lim, cmat, dvec, wglu, bglu, onorm, wout)


def _xattn_kernel(x_ref, g_ref, wq32_ref, wo32_ref, k_ref, v_ref, out_ref, qs_ref, os_ref, wq_ref, wo_ref):
    tm = x_ref.shape[0]
    tt = tm // NB

    @pl.when(pl.program_id(0) == 0)
    def _():
        wq_ref[...] = (wq32_ref[...] * (HEAD_DIM_X ** -0.5)).astype(BF16)
        wo_ref[...] = wo32_ref[...].astype(BF16)

    x = x_ref[...]
    q = _dot(_rms(x, g_ref[...]), wq_ref[...])
    n_slab = D_MODEL // LANE
    for s in range(n_slab):
        qs_ref[s] = q[:, s * LANE:(s + 1) * LANE]
    per_head = HEAD_DIM_X // LANE
    heads = [slice(h * HEAD_DIM_X, (h + 1) * HEAD_DIM_X) for h in range(N_HEADS_X)]

    def scores(b):
        rows = pl.ds(b, tt, stride=NB)
        out = []
        for h in range(N_HEADS_X):
            qbh = jnp.concatenate([qs_ref[per_head * h + i, rows, :] for i in range(per_head)], axis=1)
            out.append(_dot(qbh, k_ref[b, heads[h], :]))
        return out

    sc_next = scores(0)
    for b in range(NB):
        sc = sc_next
        if b + 1 < NB:
            sc_next = scores(b + 1)
        e = [jnp.exp(s - jnp.max(s, axis=-1, keepdims=True)) for s in sc]
        rows = pl.ds(b, tt, stride=NB)
        for h in range(N_HEADS_X):
            obh = _dot(e[h], v_ref[b, :, heads[h]]) / jnp.sum(e[h], axis=-1, keepdims=True)
            for i in range(per_head):
                os_ref[per_head * h + i, rows, :] = obh[:, i * LANE:(i + 1) * LANE]
    o = jnp.concatenate([os_ref[s] for s in range(n_slab)], axis=1)
    out_ref[...] = x + _dot(o, wo_ref[...])


def _xattn(x2, g, wq, wo, k, v, layer, tt):
    rows = x2.shape[0]
    tm = tt * NB
    row_spec = pl.BlockSpec((tm, D_MODEL), lambda i: (i, 0))
    slabs = pltpu.VMEM((D_MODEL // LANE, tm, LANE), F32)
    w_bf = pltpu.VMEM((D_MODEL, D_MODEL), BF16)
    return pl.pallas_call(
        _xattn_kernel,
        grid=(rows // tm,),
        in_specs=[row_spec, _const(g.shape)] + [_layer_const(a.shape, layer) for a in (wq, wo, k, v)],
        out_specs=row_spec,
        out_shape=jax.ShapeDtypeStruct(x2.shape, F32),
        scratch_shapes=[slabs, slabs, w_bf, w_bf],
        compiler_params=_params(1),
        name="mem_xattn",
    )(x2, g, wq, wo, k, v)


def _stream_rows(src_hbm, rows, stage_ref, sem, consume):
    n = src_hbm.shape[0] // rows
    assert n * rows == src_hbm.shape[0]

    def copy(c):
        return pltpu.make_async_copy(src_hbm.at[pl.ds(c * rows, rows), :], stage_ref.at[c % 2], sem.at[c % 2])

    copy(0).start()
    for c in range(n):
        if c + 1 < n:
            copy(c + 1).start()
        copy(c).wait()
        consume(c, c % 2)


def _ffn_main(x_ref, g_ref, wup_hbm, cg_ref, cv_ref, wdn_hbm, carry_g_ref, carry_v_ref, acc_ref, y_ref,
              wg_ref, wv_ref, wd_ref, stage_up_ref, stage_dn_ref, sem_up, sem_dn, *, layer, first, hooks=()):
    tm = x_ref.shape[0]
    hist = (CONV_FFN - 1) * NB

    if first:
        carry_g_ref[...] = jnp.zeros_like(carry_g_ref)
        carry_v_ref[...] = jnp.zeros_like(carry_v_ref)

        def up_done(c, slot):
            rows = slice(c * FFN_UP_ROWS, (c + 1) * FFN_UP_ROWS)
            wg_ref[rows, :] = stage_up_ref[slot, :, :D_FF].astype(BF16)
            wv_ref[rows, :] = stage_up_ref[slot, :, D_FF:].astype(BF16)

        def dn_done(c, slot):
            rows = slice(c * FFN_DN_ROWS, (c + 1) * FFN_DN_ROWS)
            wd_ref[rows, :] = stage_dn_ref[slot].astype(BF16)

        _stream_rows(wup_hbm.at[layer], FFN_UP_ROWS, stage_up_ref, sem_up, up_done)
        _stream_rows(wdn_hbm.at[layer], FFN_DN_ROWS, stage_dn_ref, sem_dn, dn_done)

    x = x_ref[...]
    xn = _rms(x, g_ref[...]).astype(BF16)

    def conv(pre, c_ref, carry_ref, sl):
        ext = jnp.concatenate([carry_ref[:, sl], pre], axis=0)
        carry_ref[:, sl] = pre[tm - hist:, :]
        acc = ext[hist:, :] * c_ref[CONV_FFN - 1:CONV_FFN, sl]
        for j in range(1, CONV_FFN):
            acc = acc + ext[hist - NB * j: hist - NB * j + tm, :] * c_ref[CONV_FFN - 1 - j:CONV_FFN - j, sl]
        return acc

    bounds = [sum(FF_CHUNKS[:c]) for c in range(len(FF_CHUNKS) + 1)]
    assert bounds[-1] == D_FF

    def up(c):
        sl = slice(bounds[c], bounds[c + 1])
        return _dot(xn, wg_ref[:, sl]), _dot(xn, wv_ref[:, sl])

    n_chunks = len(FF_CHUNKS)
    pre_next = up(0)
    for c in range(n_chunks):
        sl = slice(bounds[c], bounds[c + 1])
        pre_g, pre_v = pre_next
        if c + 1 < n_chunks:
            pre_next = up(c + 1)
        act = _silu(conv(pre_g, cg_ref, carry_g_ref, sl)) * conv(pre_v, cv_ref, carry_v_ref, sl)
        part = _dot(act, wd_ref[sl, :])
        if c == 0:
            acc_ref[...] = part
        else:
            acc_ref[...] += part
        if c < len(hooks):
            hooks[c]()
    assert len(hooks) <= n_chunks
    y_ref[...] = x + acc_ref[...]


def _ffn_tail(y_ref, tail_in, out_ref, tail_ref, *, tail, tile):
    tm = y_ref.shape[0]
    tt = tm // NB
    state = {}

    def norm_scale():
        y = y_ref[...]
        state["inv"] = lax.rsqrt(jnp.mean(y * y, axis=-1, keepdims=True) + RMS_EPS)

    if tail == "final":
        def slab(s):
            sl = slice(s * LANE, (s + 1) * LANE)
            tail_ref[s] = y_ref[:, sl] * state["inv"] * tail_in[0][:, sl]
            for b in range(NB):
                out_ref[b, :, sl] = tail_ref[s, pl.ds(b, tt, stride=NB), :]

        groups = [(0, 1), (2, 3), (4,), (5,), (6,), (7,)]
        pieces = [lambda ss=ss: [slab(s) for s in ss] for ss in groups]
        first = pieces[0]
        pieces[0] = lambda: (norm_scale(), first())
        return pieces

    pg_ref, pw_ref, ps_ref = tail_in
    hist = tail_ref.shape[0]

    def prepare(gi):
        if gi == 0:
            norm_scale()
            tpos = tile * tt + lax.broadcasted_iota(jnp.int32, (tm, 1), 0) // NB + 1
            state["tpos"] = tpos
        win = POOL_WINDOWS[gi]
        sl = slice(gi * POOL_GROUP, (gi + 1) * POOL_GROUP)
        xg = y_ref[:, sl] * state["inv"] * pg_ref[:, sl]
        s = jnp.concatenate([tail_ref[:, sl], xg], axis=0)
        tail_ref[:, sl] = xg[tm - hist:, :]
        span = 1
        while span < win:
            s = s[NB * span:, :] + s[:s.shape[0] - NB * span, :]
            span *= 2
        s = s[s.shape[0] - tm:, :]
        count = jnp.minimum(state["tpos"], win).astype(F32)
        state[gi] = (s * (1.0 / count) - xg).astype(BF16)

    def finish(gi):
        sl = slice(gi * POOL_GROUP, (gi + 1) * POOL_GROUP)
        out_ref[:, sl] = y_ref[:, sl] + _dot(state[gi], pw_ref[gi]) * ps_ref[:, sl]

    n = len(POOL_WINDOWS)
    pieces = [lambda: prepare(0)]
    pieces += [lambda gi=gi: (finish(gi - 1), prepare(gi)) for gi in range(1, n)]
    pieces.append(lambda: finish(n - 1))
    return pieces


def _ffn_kernel(*refs, tail, layer, n_tiles):
    n_tail_in = {"pool": 3, "final": 1}[tail]
    main_in, tail_in = refs[:6], refs[6:6 + n_tail_in]
    out_ref = refs[6 + n_tail_in]
    main_scratch, tail_ref = refs[7 + n_tail_in:-1], refs[-1]
    y_ref = main_scratch[3]
    step = pl.program_id(0)

    def tail_pieces():
        return _ffn_tail(y_ref, tail_in, out_ref, tail_ref, tail=tail, tile=step - 1)

    @pl.when(step == 0)
    def _():
        if tail == "pool":
            tail_ref[...] = jnp.zeros_like(tail_ref)
        _ffn_main(*main_in, *main_scratch, layer=layer, first=True)

    @pl.when((step > 0) & (step < n_tiles))
    def _():
        _ffn_main(*main_in, *main_scratch, layer=layer, first=False, hooks=tail_pieces())

    @pl.when(step == n_tiles)
    def _():
        for piece in tail_pieces():
            piece()


def _ffn(x2, g, w_up, cg, cv, w_down, layer, tt, tail, tail_args):
    rows = x2.shape[0]
    tm = tt * NB
    n_tiles = rows // tm
    in_row = pl.BlockSpec((tm, D_MODEL), lambda i: (jnp.minimum(i, n_tiles - 1), 0))
    hist = (CONV_FFN - 1) * NB
    if tail == "final":
        out_spec = pl.BlockSpec((NB, tt, D_MODEL), lambda i: (0, jnp.maximum(i - 1, 0), 0))
        out_shape = jax.ShapeDtypeStruct((NB, rows // NB, D_MODEL), F32)
        tail_scratch = pltpu.VMEM((D_MODEL // LANE, tm, LANE), F32)
    else:
        out_spec = pl.BlockSpec((tm, D_MODEL), lambda i: (jnp.maximum(i - 1, 0), 0))
        out_shape = jax.ShapeDtypeStruct(x2.shape, F32)
        tail_scratch = pltpu.VMEM((POOL_WINDOWS[-1] * NB, D_MODEL), F32)
    scratch = [pltpu.VMEM((hist, D_FF), F32), pltpu.VMEM((hist, D_FF), F32),
               pltpu.VMEM((tm, D_MODEL), F32), pltpu.VMEM((tm, D_MODEL), F32),
               pltpu.VMEM((D_MODEL, D_FF), BF16), pltpu.VMEM((D_MODEL, D_FF), BF16),
               pltpu.VMEM((D_FF, D_MODEL), BF16),
               pltpu.VMEM((2, FFN_UP_ROWS, 2 * D_FF), F32), pltpu.VMEM((2, FFN_DN_ROWS, D_MODEL), F32),
               pltpu.SemaphoreType.DMA((2,)), pltpu.SemaphoreType.DMA((2,)), tail_scratch]
    hbm = pl.BlockSpec(memory_space=pl.ANY)
    return pl.pallas_call(
        functools.partial(_ffn_kernel, tail=tail, layer=layer, n_tiles=n_tiles),
        grid=(n_tiles + 1,),
        in_specs=[in_row, _const(g.shape), hbm, _const(cg.shape), _const(cv.shape), hbm]
                 + [_const(a.shape) for a in tail_args],
        out_specs=out_spec,
        out_shape=out_shape,
        scratch_shapes=scratch,
        compiler_params=_params(1),
        name="conv_ffn",
    )(x2, g, w_up, cg, cv, w_down, *tail_args)


def _s5_tables(lam_re, lam_im, b_re, b_im, c_re, c_im, log_dt):
    dt = jnp.exp(log_dt)[:, None]
    mag = jnp.exp(lam_re * dt)
    ang = lam_im * dt
    lb_re, lb_im = mag * jnp.cos(ang), mag * jnp.sin(ang)
    den = lam_re * lam_re + lam_im * lam_im
    nr, ni = lb_re - 1.0, lb_im
    coef_re = (nr * lam_re + ni * lam_im) / den
    coef_im = (ni * lam_re - nr * lam_im) / den
    bb_re = coef_re[..., None] * b_re - coef_im[..., None] * b_im
    bb_im = coef_re[..., None] * b_im + coef_im[..., None] * b_re
    eye = jnp.eye(8, dtype=F32)

    def in_blocks(bb):
        bb = bb.reshape(N_SUPER, 8, SSM_STATE, SSM_GROUP)
        return jnp.einsum('jgph,gm->jghmp', bb, eye).reshape(N_SUPER, LANE, SUPER_STATE)

    def out_blocks(cc):
        cc = cc.reshape(N_SUPER, 8, SSM_GROUP, SSM_STATE)
        return jnp.einsum('jghp,gm->jgpmh', cc, eye).reshape(N_SUPER, SUPER_STATE, LANE)

    bmat = jnp.concatenate([in_blocks(bb_re), in_blocks(bb_im)], axis=2).astype(BF16)
    cmat = jnp.concatenate([out_blocks(c_re), -out_blocks(c_im)], axis=1).astype(BF16)
    return bmat, cmat, lb_re.reshape(1, -1), lb_im.reshape(1, -1)


def _pad_lanes(v, offset):
    return jnp.zeros((1, LANE), F32).at[0, offset:offset + v.shape[0]].set(v)


def kernel(x, mem, norm_mix_g, norm_xa_g, norm_ffn_g, norm_mem_g, norm_final_g, w_in_ab, conv_qkv_a, a_log_a, dt_bias_a, onorm_g_a, ssm_lambda_re, ssm_lambda_im, ssm_b_re, ssm_b_im, ssm_c_re, ssm_c_im, ssm_d, ssm_log_dt, w_glu_b, b_glu_b, w_out_ab, pool_w, pool_scale, xa_wq, xa_wkv, xa_wo, ffn_w_up, ffn_conv, ffn_w_down):
    b, t, d = x.shape
    assert b == NB and d == D_MODEL and t % CHUNK == 0
    tt = min(CHUNK, t)
    tt_xa = min(XATTN_TT, t)
    row = lambda v: v.reshape(1, -1).astype(F32)

    k_mem, v_mem = _memkv(mem, row(norm_mem_g), xa_wkv)

    for layer in range(2):
        if layer == 0:
            x2, q, k, v, bg, gate, u = _mix_in(
                x, row(norm_mix_g[0]), w_in_ab[0], conv_qkv_a[0],
                _pad_lanes(a_log_a[0], N_HEADS_A), _pad_lanes(dt_bias_a[0], N_HEADS_A), min(MIX_IN_TT, t))
            o = _delta(q, k, v, bg)
            bmat, cmat, lre, lim = _s5_tables(ssm_lambda_re[0], ssm_lambda_im[0], ssm_b_re[0], ssm_b_im[0],
                                              ssm_c_re[0], ssm_c_im[0], ssm_log_dt[0])
            x2 = _s5_out(x2, u, o, gate, bmat, lre, lim, cmat, row(ssm_d[0]), w_glu_b[0],
                         row(b_glu_b[0]), row(onorm_g_a[0]), w_out_ab[0], tt)
        x2 = _xattn(x2, row(norm_xa_g[layer]), xa_wq, xa_wo, k_mem, v_mem, layer, tt_xa)
        tail, tail_args = (("pool", (row(norm_mix_g[1]), pool_w[0].astype(BF16), row(pool_scale[0])))
                           if layer == 0 else ("final", (row(norm_final_g),)))
        x2 = _ffn(x2, row(norm_ffn_g[layer]), ffn_w_up, ffn_conv[layer][:, :D_FF], ffn_conv[layer][:, D_FF:],
                  ffn_w_down, layer, min(FFN_TT, t), tail=tail, tail_args=tail_args)
    return x2
```
